```python
import math
import jax, jax.numpy as jnp
from jax import lax
import numpy as np

D_MODEL = 1024
BATCH = 2
SEQ = 8192
DEPTH = 4
DEC_BATCH = 16
DEC_SEQ = 32
PAST_LEN = 2048

CHUNK = 64
QBLOCK = 128
HEAD_DIM = 64
N_RET = 6
N_FOX = 6
N_DIFF = 6
DIFF_QK = HEAD_DIM // 2
RET_W = N_RET * HEAD_DIM
FOX_W = N_FOX * HEAD_DIM
DIFF_W = N_DIFF * HEAD_DIM
MIX_W = RET_W + FOX_W + DIFF_W
IN_W = 4 * RET_W + 3 * FOX_W + N_FOX + 3 * DIFF_W
D_FF = 4 * D_MODEL
ROPE_THETA = 10000.0
EPS = 1e-6
FORGET_BIAS = 3.0

kernel_name = 'hybrid_streaming_encoder_step'


def _split_points():
    sizes = [RET_W] * 4 + [FOX_W] * 3 + [N_FOX] + [DIFF_W] * 3
    return [int(s) for s in np.cumsum(sizes)[:-1]]


def _rmsnorm(x, g):
    xf = x.astype(jnp.float32)
    y = xf * lax.rsqrt(jnp.mean(xf * xf, axis=-1, keepdims=True) + EPS)
    return (y * g.astype(jnp.float32)).astype(x.dtype)


def _group_norm(x, g, b):
    B, L = x.shape[0], x.shape[1]
    xf = x.astype(jnp.float32)
    mu = jnp.mean(xf, axis=-1, keepdims=True)
    xc = xf - mu
    y = (xc * lax.rsqrt(jnp.mean(xc * xc, axis=-1, keepdims=True) + EPS)).reshape(B, L, -1)
    return (y * g.astype(jnp.float32) + b.astype(jnp.float32)).astype(x.dtype)


def _head_rmsnorm(x, g):
    B, L = x.shape[0], x.shape[1]
    xf = x.astype(jnp.float32)
    y = (xf * lax.rsqrt(jnp.mean(xf * xf, axis=-1, keepdims=True) + EPS)).reshape(B, L, -1)
    return (y * g.astype(jnp.float32)).astype(x.dtype)


def _rope(x, pos):
    half = x.shape[-1] // 2
    inv = ROPE_THETA ** (-jnp.arange(half, dtype=jnp.float32) / half)
    ang = pos.astype(jnp.float32)[:, None] * inv[None, :]
    shp = (1, x.shape[1]) + (1,) * (x.ndim - 3) + (half,)
    cos = jnp.cos(ang).reshape(shp)
    sin = jnp.sin(ang).reshape(shp)
    xf = x.astype(jnp.float32)
    x1, x2 = xf[..., :half], xf[..., half:]
    return jnp.concatenate([x1 * cos - x2 * sin, x1 * sin + x2 * cos], axis=-1).astype(x.dtype)


def _retention(q, k, v, s0):
    B, L, H, dk = q.shape
    C = min(CHUNK, L)
    n = L // C
    log_g = jnp.log1p(-jnp.exp2(-5.0 - jnp.arange(H, dtype=jnp.float32)))
    qc = q.astype(jnp.float32).reshape(B, n, C, H, dk)
    kc = k.astype(jnp.float32).reshape(B, n, C, H, dk) * (dk ** -0.5)
    vc = v.astype(jnp.float32).reshape(B, n, C, H, -1)
    idx = jnp.arange(C, dtype=jnp.float32)
    intra_decay = jnp.exp(log_g[:, None, None] * jnp.abs(idx[:, None] - idx[None, :]))
    scores = jnp.einsum('bnihd,bnjhd->bnhij', qc, kc) * intra_decay
    intra = jnp.einsum('bnhij,bnjhe->bnihe', scores, vc)
    k_w = jnp.exp((C - 1.0 - idx)[:, None] * log_g[None, :])
    u = jnp.einsum('bnjhd,bnjhe->nbhde', kc * k_w[:, :, None], vc)
    chunk_decay = jnp.exp(C * log_g)[:, None, None]

    def step(s, u_c):
        return chunk_decay * s + u_c, s

    s_last, s_before = lax.scan(step, s0.astype(jnp.float32), u)
    q_w = jnp.exp((idx + 1.0)[:, None] * log_g[None, :])
    cross = jnp.einsum('bnihd,nbhde->bnihe', qc * q_w[:, :, None], s_before)
    out = (intra + cross).reshape(B, L, H, -1)
    return out.astype(q.dtype), s_last


def _fox_attention(q, k, v, fq, fk, q_pos, k_pos):
    B, Lq, H, d = q.shape
    qb_len = min(QBLOCK, Lq)
    nb = Lq // qb_len
    scale = d ** -0.5
    q_blocks = q.reshape(B, nb, qb_len, H, d).transpose(1, 0, 2, 3, 4)
    f_blocks = fq.reshape(B, nb, qb_len, H).transpose(1, 0, 2, 3)
    p_blocks = q_pos.reshape(nb, qb_len)
    fk_t = jnp.transpose(fk, (0, 2, 1))[:, :, None, :]

    def block(args):
        qq, fqq, pq = args
        s = jnp.einsum('bqhd,bkhd->bhqk', qq, k, preferred_element_type=jnp.float32) * scale
        s = s + jnp.transpose(fqq, (0, 2, 1))[:, :, :, None] - fk_t
        mask = k_pos[None, :] <= pq[:, None]
        p = jax.nn.softmax(jnp.where(mask, s, -jnp.inf), axis=-1)
        return jnp.einsum('bhqk,bkhd->bqhd', p.astype(v.dtype), v)

    out = lax.map(block, (q_blocks, f_blocks, p_blocks))
    return out.transpose(1, 0, 2, 3, 4).reshape(B, Lq, H, d)


def _diff_attention(q, k, v, lam, q_pos, k_pos):
    B, Lq, H, _, dq = q.shape
    qb_len = min(QBLOCK, Lq)
    nb = Lq // qb_len
    scale = dq ** -0.5
    q_blocks = q.reshape(B, nb, qb_len, H, 2, dq).transpose(1, 0, 2, 3, 4, 5)
    p_blocks = q_pos.reshape(nb, qb_len)
    k_chunk = k_pos // CHUNK

    def block(args):
        qq, pq = args
        s = jnp.einsum('bqhcd,bkhcd->cbhqk', qq, k, preferred_element_type=jnp.float32) * scale
        mask = k_chunk[None, :] <= (pq // CHUNK)[:, None]
        p = jax.nn.softmax(jnp.where(mask, s, -jnp.inf), axis=-1)
        a = p[0] - lam * p[1]
        return jnp.einsum('bhqk,bkhd->bqhd', a.astype(v.dtype), v)

    out = lax.map(block, (q_blocks, p_blocks))
    return out.transpose(1, 0, 2, 3, 4).reshape(B, Lq, H, -1)


def _token_mixers(h, pos, k_pos, past, lam_init, w_in, b_forget, ret_norm_g, ret_norm_b,
                  lam_q1, lam_k1, lam_q2, lam_k2, diff_norm_g):
    B, L, _ = h.shape
    proj = h @ w_in
    rq, rk, rv, rg, fq, fk, fv, ff, dq, dk, dv = jnp.split(proj, _split_points(), axis=-1)
    rq = _rope(rq.reshape(B, L, N_RET, HEAD_DIM), pos)
    rk = _rope(rk.reshape(B, L, N_RET, HEAD_DIM), pos)
    rv = rv.reshape(B, L, N_RET, HEAD_DIM)
    fq = fq.reshape(B, L, N_FOX, HEAD_DIM)
    fk = fk.reshape(B, L, N_FOX, HEAD_DIM)
    fv = fv.reshape(B, L, N_FOX, HEAD_DIM)
    logf = jax.nn.log_sigmoid(ff.astype(jnp.float32) + b_forget.astype(jnp.float32))
    dq = _rope(dq.reshape(B, L, N_DIFF, 2, DIFF_QK), pos)
    dk = _rope(dk.reshape(B, L, N_DIFF, 2, DIFF_QK), pos).reshape(B, L, N_DIFF, HEAD_DIM)
    dv = dv.reshape(B, L, N_DIFF, HEAD_DIM)
    if past is None:
        s0 = jnp.zeros((B, N_RET, HEAD_DIM, HEAD_DIM), jnp.float32)
        fk_all, fv_all, logf_all, dk_all, dv_all = fk, fv, logf, dk, dv
    else:
        pfk, pfv, plogf, pdk, pdv, s0 = past
        fk_all = jnp.concatenate([pfk.astype(fk.dtype), fk], axis=1)
        fv_all = jnp.concatenate([pfv.astype(fv.dtype), fv], axis=1)
        logf_all = jnp.concatenate([plogf.astype(jnp.float32), logf], axis=1)
        dk_all = jnp.concatenate([pdk.astype(dk.dtype), dk], axis=1)
        dv_all = jnp.concatenate([pdv.astype(dv.dtype), dv], axis=1)
    r_out, s_new = _retention(rq, rk, rv, s0)
    r_out = jax.nn.silu(rg) * _group_norm(r_out, ret_norm_g, ret_norm_b)
    f_cum = jnp.cumsum(logf_all, axis=1)
    f_out = _fox_attention(fq, fk_all, fv_all, f_cum[:, -L:], f_cum, pos, k_pos)
    lq1, lk1 = lam_q1.astype(jnp.float32), lam_k1.astype(jnp.float32)
    lq2, lk2 = lam_q2.astype(jnp.float32), lam_k2.astype(jnp.float32)
    lam = jnp.exp(jnp.sum(lq1 * lk1)) - jnp.exp(jnp.sum(lq2 * lk2)) + lam_init
    d_out = _diff_attention(dq, dk_all.reshape(B, -1, N_DIFF, 2, DIFF_QK), dv_all, lam, pos, k_pos)
    d_out = _head_rmsnorm(d_out, diff_norm_g) * (1.0 - lam_init)
    mix = jnp.concatenate([r_out, f_out.reshape(B, L, FOX_W), d_out.astype(r_out.dtype)], axis=-1)
    return mix, (fk, fv, logf, dk, dv, s_new)


def _trunk_layer(x, c, pos, k_pos, past, lam_init, norm1_g, norm2_g, w_ada, b_ada, w_in, b_forget,
                 ret_norm_g, ret_norm_b, lam_q1, lam_k1, lam_q2, lam_k2, diff_norm_g, w_out, w_up, w_down):
    mod = jax.nn.silu(c) @ w_ada + b_ada
    sh1, sc1, g1, sh2, sc2, g2 = jnp.split(mod[:, None, :], 6, axis=-1)
    h = _rmsnorm(x, norm1_g) * (1.0 + sc1) + sh1
    mix, new_state = _token_mixers(h, pos, k_pos, past, lam_init, w_in, b_forget, ret_norm_g, ret_norm_b,
                                   lam_q1, lam_k1, lam_q2, lam_k2, diff_norm_g)
    x = x + g1 * (mix @ w_out)
    h = _rmsnorm(x, norm2_g) * (1.0 + sc2) + sh2
    x = x + g2 * (jnp.square(jax.nn.relu(h @ w_up)) @ w_down)
    return x, new_state


def setup_inputs(seed: int = 0) -> dict:
    key = jax.random.key(seed)
    ks = jax.random.split(key, 32)
    f32 = jnp.float32

    def nrm(k, shape, scale):
        return jax.random.normal(k, shape, f32) * scale

    return {
        'x_prompt': nrm(ks[0], (BATCH, SEQ, D_MODEL), 1.0),
        'x_sample': nrm(ks[1], (DEC_BATCH, DEC_SEQ, D_MODEL), 1.0),
        'cache_fox_k': nrm(ks[2], (DEPTH, DEC_BATCH, PAST_LEN, N_FOX, HEAD_DIM), 1.0),
        'cache_fox_v': nrm(ks[3], (DEPTH, DEC_BATCH, PAST_LEN, N_FOX, HEAD_DIM), 1.0),
        'cache_fox_logf': jax.nn.log_sigmoid(nrm(ks[4], (DEPTH, DEC_BATCH, PAST_LEN, N_FOX), 1.0) + FORGET_BIAS),
        'cache_diff_k': nrm(ks[5], (DEPTH, DEC_BATCH, PAST_LEN, N_DIFF, HEAD_DIM), 1.0),
        'cache_diff_v': nrm(ks[6], (DEPTH, DEC_BATCH, PAST_LEN, N_DIFF, HEAD_DIM), 1.0),
        'state_ret': nrm(ks[7], (DEPTH, DEC_BATCH, N_RET, HEAD_DIM, HEAD_DIM), 0.5),
        'c_prompt': nrm(ks[8], (BATCH, D_MODEL), 1.0),
        'c_sample': nrm(ks[9], (DEC_BATCH, D_MODEL), 1.0),
        'norm1_g': 1.0 + nrm(ks[10], (DEPTH, D_MODEL), 0.05),
        'norm2_g': 1.0 + nrm(ks[11], (DEPTH, D_MODEL), 0.05),
        'w_ada': nrm(ks[12], (DEPTH, D_MODEL, 6 * D_MODEL), D_MODEL ** -0.5),
        'b_ada': nrm(ks[13], (DEPTH, 6 * D_MODEL), 0.02),
        'w_in': nrm(ks[14], (DEPTH, D_MODEL, IN_W), D_MODEL ** -0.5),
        'b_forget': FORGET_BIAS + nrm(ks[15], (DEPTH, N_FOX), 0.1),
        'ret_norm_g': 1.0 + nrm(ks[16], (DEPTH, RET_W), 0.05),
        'ret_norm_b': nrm(ks[17], (DEPTH, RET_W), 0.02),
        'lam_q1': nrm(ks[18], (DEPTH, DIFF_QK), 0.1),
        'lam_k1': nrm(ks[19], (DEPTH, DIFF_QK), 0.1),
        'lam_q2': nrm(ks[20], (DEPTH, DIFF_QK), 0.1),
        'lam_k2': nrm(ks[21], (DEPTH, DIFF_QK), 0.1),
        'diff_norm_g': 1.0 + nrm(ks[22], (DEPTH, DIFF_W), 0.05),
        'w_out': nrm(ks[23], (DEPTH, MIX_W, D_MODEL), MIX_W ** -0.5),
        'w_up': nrm(ks[24], (DEPTH, D_MODEL, D_FF), D_MODEL ** -0.5),
        'w_down': nrm(ks[25], (DEPTH, D_FF, D_MODEL), D_FF ** -0.5),
        'final_g': 1.0 + nrm(ks[26], (D_MODEL,), 0.05),
    }


def reference(x_prompt, x_sample, cache_fox_k, cache_fox_v, cache_fox_logf, cache_diff_k, cache_diff_v,
              state_ret, c_prompt, c_sample, norm1_g, norm2_g, w_ada, b_ada, w_in, b_forget,
              ret_norm_g, ret_norm_b, lam_q1, lam_k1, lam_q2, lam_k2, diff_norm_g, w_out, w_up, w_down,
              final_g):
    lp = x_prompt.shape[1]
    ls = x_sample.shape[1]
    past_len = cache_fox_k.shape[2]
    p_pos = jnp.arange(lp, dtype=jnp.int32)
    s_pos = past_len + jnp.arange(ls, dtype=jnp.int32)
    s_kpos = jnp.arange(past_len + ls, dtype=jnp.int32)
    xp, xs = x_prompt, x_sample
    p_new, s_new = [], []
    for li in range(DEPTH):
        lam_init = 0.8 - 0.6 * math.exp(-0.3 * li)
        layer_w = (norm1_g[li], norm2_g[li], w_ada[li], b_ada[li], w_in[li], b_forget[li],
                   ret_norm_g[li], ret_norm_b[li], lam_q1[li], lam_k1[li], lam_q2[li], lam_k2[li],
                   diff_norm_g[li], w_out[li], w_up[li], w_down[li])
        xp, ps = _trunk_layer(xp, c_prompt, p_pos, p_pos, None, lam_init, *layer_w)
        past = (cache_fox_k[li], cache_fox_v[li], cache_fox_logf[li], cache_diff_k[li], cache_diff_v[li],
                state_ret[li])
        xs, ss = _trunk_layer(xs, c_sample, s_pos, s_kpos, past, lam_init, *layer_w)
        p_new.append(ps)
        s_new.append(ss)
    y_prompt = _rmsnorm(xp, final_g)
    y_sample = _rmsnorm(xs, final_g)
    p_fox_k = jnp.stack([s[0] for s in p_new])
    p_fox_v = jnp.stack([s[1] for s in p_new])
    p_fox_logf = jnp.stack([s[2] for s in p_new])
    p_diff_k = jnp.stack([s[3] for s in p_new])
    p_diff_v = jnp.stack([s[4] for s in p_new])
    p_ret_state = jnp.stack([s[5] for s in p_new])
    s_fox_k = jnp.stack([s[0] for s in s_new])
    s_fox_v = jnp.stack([s[1] for s in s_new])
    s_fox_logf = jnp.stack([s[2] for s in s_new])
    s_diff_k = jnp.stack([s[3] for s in s_new])
    s_diff_v = jnp.stack([s[4] for s in s_new])
    s_ret_state = jnp.stack([s[5] for s in s_new])
    return (y_prompt, y_sample, p_fox_k, p_fox_v, p_fox_logf, p_diff_k, p_diff_v, p_ret_state,
            s_fox_k, s_fox_v, s_fox_logf, s_diff_k, s_diff_v, s_ret_state)
```

```python
import functools
import math

import jax
import jax.numpy as jnp
from jax import lax
from jax.experimental import pallas as pl
from jax.experimental.pallas import tpu as pltpu

F32 = jnp.float32
BF16 = jnp.bfloat16

CHUNK = 64
HEAD_DIM = 64
N_HEADS = 6
GROUP_W = N_HEADS * HEAD_DIM
LANES = 128
N_PAIRS = GROUP_W // LANES
DIFF_QK = HEAD_DIM // 2
ROPE_THETA = 10000.0
EPS = 1e-6
MASKED = -1e30
LOG_G = tuple(math.log1p(-(2.0 ** (-5.0 - h))) for h in range(N_HEADS))
TOK_TILE = 512
ATT_TILE = 512
F_ROWS = 16
VMEM_LIMIT = 56 * 1024 * 1024


def _params(*sem):
    return pltpu.CompilerParams(dimension_semantics=sem, vmem_limit_bytes=VMEM_LIMIT)


def _resident(shape):
    zeros = (0,) * len(shape)
    return pl.BlockSpec(shape, lambda *_: zeros, pipeline_mode=pl.Buffered(1))


def _sigmoid(x):
    return 1.0 / (1.0 + jnp.exp(-x))


def _log2(n):
    assert n > 0 and n & (n - 1) == 0, n
    return n.bit_length() - 1


def _dot_nt(a, b):
    return lax.dot_general(a, b, (((1,), (1,)), ((), ())), preferred_element_type=F32)


def _dot_tn(a, b):
    return lax.dot_general(a, b, (((0,), (0,)), ((), ())), preferred_element_type=F32)


def _rms_mod(x, g, scale, shift):
    y = x * lax.rsqrt(jnp.mean(x * x, axis=-1, keepdims=True) + EPS)
    return (y * g) * (1.0 + scale) + shift


def _lower_half(shape):
    return lax.broadcasted_iota(jnp.int32, shape, len(shape) - 1) < HEAD_DIM


def _half_sums(x, lo):
    a = jnp.sum(jnp.where(lo, x, 0.0), axis=-1, keepdims=True)
    b = jnp.sum(jnp.where(lo, 0.0, x), axis=-1, keepdims=True)
    return jnp.where(lo, a, b)


def _ada_kernel(c_ref, w_ref, b_ref, o_ref):
    c = c_ref[...]
    a = (c * _sigmoid(c)).astype(BF16)
    o_ref[0] = jnp.dot(a, w_ref[0].astype(BF16), preferred_element_type=F32) + b_ref[0]


def _ada(c_all, w_ada, b_ada):
    depth, d, n = w_ada.shape
    rows = c_all.shape[0]
    tn = n // 4
    return pl.pallas_call(
        _ada_kernel,
        grid=(depth, n // tn),
        in_specs=[
            pl.BlockSpec((rows, d), lambda l, j: (0, 0)),
            pl.BlockSpec((1, d, tn), lambda l, j: (l, 0, j)),
            pl.BlockSpec((1, 1, tn), lambda l, j: (l, 0, j)),
        ],
        out_specs=pl.BlockSpec((1, rows, tn), lambda l, j: (l, 0, j)),
        out_shape=jax.ShapeDtypeStruct((depth, rows, n), F32),
        compiler_params=_params("arbitrary", "arbitrary"),
        name="ada_mod",
    )(c_all, w_ada, b_ada.reshape(depth, 1, n))


def _rope(y, cos, sin_signed, half):
    width = y.shape[-1]
    lane = lax.broadcasted_iota(jnp.int32, y.shape, 1)
    ahead = pltpu.roll(y, width - half, axis=1)
    behind = pltpu.roll(y, half, axis=1)
    swapped = jnp.where((lane & half) == 0, ahead, behind)
    return y * cos + swapped * sin_signed


def _inproj_kernel(x_ref, sc_ref, sh_ref, g_ref, wm_ref, wf_ref, bf_ref,
                   c64_ref, s64_ref, c32_ref, s32_ref,
                   rq_o, rk_o, rv_o, rg_o, fq_o, fkf_o, fkb_o, fvf_o, fvb_o, lf_o,
                   dq_o, dkf_o, dkb_o, dvf_o, dvb_o):
    bt, tl, d = x_ref.shape
    m = bt * tl
    h = _rms_mod(x_ref[...], g_ref[...], sc_ref[...], sh_ref[...])
    hb = h.reshape(m, d).astype(BF16)

    def piece(i):
        return jnp.dot(hb, wm_ref[:, i * GROUP_W:(i + 1) * GROUP_W], preferred_element_type=F32)

    def put(y, *refs):
        y = y.reshape(bt, tl, y.shape[-1])
        for ref in refs:
            ref[...] = y.astype(ref.dtype)

    c64, s64 = c64_ref[...], s64_ref[...]
    c32, s32 = c32_ref[...], s32_ref[...]
    put(_rope(piece(0), c64, s64, HEAD_DIM // 2), rq_o)
    put(_rope(piece(1), c64, s64, HEAD_DIM // 2), rk_o)
    put(piece(2), rv_o)
    put(piece(3), rg_o)
    put(piece(4) * HEAD_DIM ** -0.5, fq_o)
    put(piece(5), fkf_o, fkb_o)
    put(piece(6), fvf_o, fvb_o)
    z = jnp.dot(hb, wf_ref[...], preferred_element_type=F32) + bf_ref[...]
    put(jnp.minimum(z, 0.0) - jnp.log1p(jnp.exp(-jnp.abs(z))), lf_o)
    put(_rope(piece(7), c32, s32, DIFF_QK // 2) * DIFF_QK ** -0.5, dq_o)
    put(_rope(piece(8), c32, s32, DIFF_QK // 2), dkf_o, dkb_o)
    put(piece(9), dvf_o, dvb_o)


def _inproj(x, scale, shift, gain, w_main, w_ff, b_ff, tabs, bt, tl):
    b, l, d = x.shape
    m = bt * tl
    grid = (b // bt, l // tl)
    tile = lambda w: pl.BlockSpec((bt, tl, w), lambda i, t: (i, t, 0))
    mod = pl.BlockSpec((bt, 1, d), lambda i, t: (i, 0, 0))
    tab = pl.BlockSpec((m, GROUP_W), lambda i, t: (t, 0))
    out = lambda w, dt: jax.ShapeDtypeStruct((b, l, w), dt)
    outs = [
        (GROUP_W, BF16), (GROUP_W, BF16), (GROUP_W, BF16), (GROUP_W, F32),
        (GROUP_W, BF16), (GROUP_W, F32), (GROUP_W, BF16), (GROUP_W, F32), (GROUP_W, BF16),
        (LANES, F32),
        (GROUP_W, BF16), (GROUP_W, F32), (GROUP_W, BF16), (GROUP_W, F32), (GROUP_W, BF16),
    ]
    return pl.pallas_call(
        _inproj_kernel,
        grid=grid,
        in_specs=[tile(d), mod, mod, _resident((1, d)), _resident(w_main.shape),
                  _resident(w_ff.shape), _resident((1, LANES)), tab, tab, tab, tab],
        out_specs=[tile(w) for w, _ in outs],
        out_shape=[out(w, dt) for w, dt in outs],
        compiler_params=_params("arbitrary", "arbitrary"),
        name="in_proj",
    )(x, scale, shift, gain, w_main, w_ff, b_ff, *tabs)


def _cumsum_kernel(x_ref, o_ref):
    rows, n = x_ref.shape[1], x_ref.shape[2]
    r = lax.broadcasted_iota(jnp.int32, (LANES, LANES), 0)
    c = lax.broadcasted_iota(jnp.int32, (LANES, LANES), 1)
    upper = (r <= c).astype(BF16)
    carry = jnp.zeros((rows, 1), F32)
    for j in range(n // LANES):
        x = x_ref[0, :, j * LANES:(j + 1) * LANES]
        hi = x.astype(BF16)
        r1 = x - hi.astype(F32)
        mid = r1.astype(BF16)
        lo = (r1 - mid.astype(F32)).astype(BF16)
        y = jnp.dot(jnp.concatenate([hi, mid, lo], axis=0), upper, preferred_element_type=F32)
        y = y[:rows] + y[rows:2 * rows] + y[2 * rows:]
        o_ref[0, :, j * LANES:(j + 1) * LANES] = y + carry
        carry = carry + y[:, LANES - 1:LANES]


def _cumsum(logf_t):
    b, rows, n = logf_t.shape
    spec = pl.BlockSpec((1, rows, n), lambda i: (i, 0, 0))
    return pl.pallas_call(
        _cumsum_kernel, grid=(b,), in_specs=[spec], out_specs=spec,
        out_shape=jax.ShapeDtypeStruct(logf_t.shape, F32),
        compiler_params=_params("arbitrary"),
        name="forget_cumsum",
    )(logf_t)


def _ret_kernel(q_ref, k_ref, v_ref, g_ref, s0_ref, ng_ref, nb_ref, o_ref, sn_ref, w_scr, s_scr,
                *, chunk):
    bt, tl, _ = q_ref.shape
    m = bt * tl
    t = pl.program_id(1)

    @pl.when((pl.program_id(0) == 0) & (t == 0))
    def _():
        i = lax.broadcasted_iota(jnp.int32, (m, m), 0)
        j = lax.broadcasted_iota(jnp.int32, (m, m), 1)
        visible = ((i >> _log2(tl)) == (j >> _log2(tl))) & ((j >> _log2(chunk)) <= (i >> _log2(chunk)))
        dist = jnp.abs(i - j).astype(F32)
        for h in range(N_HEADS):
            w_scr[h] = jnp.where(visible, jnp.exp(LOG_G[h] * dist), 0.0)

    @pl.when(t == 0)
    def _():
        s_scr[...] = s0_ref[...]

    head = lax.broadcasted_iota(jnp.int32, (1, GROUP_W), 1) >> _log2(HEAD_DIM)
    log_g = jnp.full((1, GROUP_W), LOG_G[N_HEADS - 1], F32)
    for h in range(N_HEADS - 1):
        log_g = jnp.where(head == h, LOG_G[h], log_g)
    pos = (lax.broadcasted_iota(jnp.int32, (m, 1), 0) & (tl - 1)).astype(F32)
    q_w = jnp.exp((pos + 1.0) * log_g)
    k_w = jnp.exp((tl - 1.0 - pos) * log_g) * HEAD_DIM ** -0.5

    qf = q_ref[...].reshape(m, GROUP_W).astype(F32)
    kf = k_ref[...].reshape(m, GROUP_W).astype(F32)
    v = v_ref[...].reshape(m, GROUP_W)
    gate = g_ref[...].reshape(m, GROUP_W)
    qwb = (qf * q_w).astype(BF16)
    kwb = (kf * k_w).astype(BF16)
    ksb = (kf * HEAD_DIM ** -0.5).astype(BF16)
    lo = _lower_half((m, LANES))
    r = lax.broadcasted_iota(jnp.int32, (LANES, LANES), 0) < HEAD_DIM
    c = lax.broadcasted_iota(jnp.int32, (LANES, LANES), 1) < HEAD_DIM
    same_head = r == c

    for p in range(N_PAIRS):
        sl = slice(p * LANES, (p + 1) * LANES)
        qp, vp = qf[:, sl], v[:, sl]
        q2 = jnp.concatenate([jnp.where(lo, qp, 0.0), jnp.where(lo, 0.0, qp)], axis=0).astype(BF16)
        s = _dot_nt(q2, ksb[:, sl])
        probs = jnp.concatenate([(s[:m] * w_scr[2 * p]).astype(BF16),
                                 (s[m:] * w_scr[2 * p + 1]).astype(BF16)], axis=0)
        o2 = jnp.dot(probs, vp, preferred_element_type=F32)
        intra = jnp.where(lo, o2[:m], o2[m:])
        tile_decay = jnp.where(r, math.exp(tl * LOG_G[2 * p]), math.exp(tl * LOG_G[2 * p + 1]))
        cross = []
        for bi in range(bt):
            rows = slice(bi * tl, (bi + 1) * tl)
            state = s_scr[bi, p]
            cross.append(jnp.dot(qwb[rows, sl], state.astype(BF16), preferred_element_type=F32))
            u = _dot_tn(kwb[rows, sl], vp[rows])
            s_scr[bi, p] = tile_decay * state + jnp.where(same_head, u, 0.0)
        y = intra + (jnp.concatenate(cross, axis=0) if bt > 1 else cross[0])
        yc = y - _half_sums(y, lo) * (1.0 / HEAD_DIM)
        var = _half_sums(yc * yc, lo) * (1.0 / HEAD_DIM)
        yn = yc * lax.rsqrt(var + EPS) * ng_ref[:, sl] + nb_ref[:, sl]
        gp = gate[:, sl]
        out = gp * _sigmoid(gp) * yn
        o_ref[:, :, sl] = out.reshape(bt, tl, LANES).astype(o_ref.dtype)

    @pl.when(t == pl.num_programs(1) - 1)
    def _():
        sn_ref[...] = s_scr[...]


def _retention(rq, rk, rv, rg, s0_pairs, norm_g, norm_b, bt, tl, chunk):
    b, l, _ = rq.shape
    m = bt * tl
    tile = pl.BlockSpec((bt, tl, GROUP_W), lambda i, t: (i, t, 0))
    st = pl.BlockSpec((bt, N_PAIRS, LANES, LANES), lambda i, t: (i, 0, 0, 0))
    return pl.pallas_call(
        functools.partial(_ret_kernel, chunk=chunk),
        grid=(b // bt, l // tl),
        in_specs=[tile, tile, tile, tile, st, _resident((1, GROUP_W)), _resident((1, GROUP_W))],
        out_specs=[tile, st],
        out_shape=[jax.ShapeDtypeStruct((b, l, GROUP_W), BF16),
                   jax.ShapeDtypeStruct(s0_pairs.shape, F32)],
        scratch_shapes=[pltpu.VMEM((N_HEADS, m, m), F32),
                        pltpu.VMEM((bt, N_PAIRS, LANES, LANES), F32)],
        compiler_params=_params("arbitrary", "arbitrary"),
        name="retention",
    )(rq, rk, rv, rg, s0_pairs, norm_g, norm_b)


def _state_to_pairs(s):
    b = s.shape[0]
    s = s.reshape(b, N_PAIRS, 2, HEAD_DIM, HEAD_DIM)
    z = jnp.zeros_like(s[:, :, 0])
    top = jnp.concatenate([s[:, :, 0], z], axis=-1)
    bot = jnp.concatenate([z, s[:, :, 1]], axis=-1)
    return jnp.concatenate([top, bot], axis=-2)


def _pairs_to_state(sp):
    b = sp.shape[0]
    a = sp[:, :, :HEAD_DIM, :HEAD_DIM]
    d = sp[:, :, HEAD_DIM:, HEAD_DIM:]
    return jnp.stack([a, d], axis=2).reshape(b, N_HEADS, HEAD_DIM, HEAD_DIM)


def _softmax_step(s, m, l):
    m_new = jnp.maximum(m, jnp.max(s, axis=-1, keepdims=True))
    alpha = jnp.exp(m - m_new)
    p = jnp.exp(s - m_new)
    return p, m_new, alpha * l + jnp.sum(p, axis=-1, keepdims=True), alpha


def _stack_masked(qp, n_streams):
    width = LANES // n_streams
    owner = lax.broadcasted_iota(jnp.int32, qp.shape, 1) >> _log2(width)
    return jnp.concatenate([jnp.where(owner == i, qp, 0.0) for i in range(n_streams)],
                           axis=0).astype(BF16)


def _fox_pair(qp, kp, vp, fq_cols, fk_rows, mask, state):
    tq = qp.shape[0]
    s = _dot_nt(_stack_masked(qp, 2), kp)
    probs, new = [], []
    for i in range(2):
        si = s[i * tq:(i + 1) * tq] + (fq_cols[i] - fk_rows[i])
        if mask is not None:
            si = jnp.where(mask, si, MASKED)
        p, m_new, l_new, alpha = _softmax_step(si, state[i][0], state[i][1])
        probs.append(p.astype(BF16))
        new.append((m_new, l_new, alpha))
    pv = jnp.dot(jnp.concatenate(probs, axis=0), vp, preferred_element_type=F32)
    return [(m_new, l_new, alpha * state[i][2] + pv[i * tq:(i + 1) * tq])
            for i, (m_new, l_new, alpha) in enumerate(new)]


def _diff_pair(qp, kp, vp, mask, state):
    tq = qp.shape[0]
    s = _dot_nt(_stack_masked(qp, 4), kp)
    probs, new = [], []
    for i in range(4):
        si = s[i * tq:(i + 1) * tq]
        if mask is not None:
            si = jnp.where(mask, si, MASKED)
        p, m_new, l_new, alpha = _softmax_step(si, state[i][0], state[i][1])
        probs.append(p.astype(BF16))
        new.append((m_new, l_new, alpha))
    pv = jnp.dot(jnp.concatenate(probs, axis=0), vp, preferred_element_type=F32)
    return [(m_new, l_new, alpha * state[i][2] + pv[i * tq:(i + 1) * tq])
            for i, (m_new, l_new, alpha) in enumerate(new)]


def _fresh(n, tq):
    return [(jnp.full((tq, 1), MASKED, F32), jnp.zeros((tq, 1), F32), jnp.zeros((tq, LANES), F32))
            for _ in range(n)]


def _fox_out(state, lo):
    (_, la, acca), (_, lb, accb) = state
    return jnp.where(lo, acca / la, accb / lb)


def _lam(lam_ref, lam_init):
    a = lam_ref[...]
    t1 = jnp.sum(a[0:1] * a[1:2], axis=-1, keepdims=True)
    t2 = jnp.sum(a[2:3] * a[3:4], axis=-1, keepdims=True)
    return jnp.exp(t1) - jnp.exp(t2) + lam_init


def _diff_out(state, lo, lam, gain, lam_init):
    (_, l0, a0), (_, l1, a1), (_, l2, a2), (_, l3, a3) = state
    o = jnp.where(lo, a0 / l0, a2 / l2) - lam * jnp.where(lo, a1 / l1, a3 / l3)
    ms = _half_sums(o * o, lo) * (1.0 / HEAD_DIM)
    return (o * lax.rsqrt(ms + EPS) * gain) * (1.0 - lam_init)


def _load_state(m_scr, l_scr, acc_scr, idx):
    return [(m_scr[i], l_scr[i], acc_scr[i]) for i in idx]


def _store_state(m_scr, l_scr, acc_scr, idx, state):
    for i, (m, l, acc) in zip(idx, state):
        m_scr[i] = m
        l_scr[i] = l
        acc_scr[i] = acc


def _init_state(m_scr, l_scr, acc_scr):
    m_scr[...] = jnp.full(m_scr.shape, MASKED, F32)
    l_scr[...] = jnp.zeros(l_scr.shape, F32)
    acc_scr[...] = jnp.zeros(acc_scr.shape, F32)


def _fox_kernel(q_ref, k_ref, v_ref, fq_ref, fk_ref, o_ref, m_scr, l_scr, acc_scr):
    tq, tk = q_ref.shape[1], k_ref.shape[1]
    qi, ki = pl.program_id(1), pl.program_id(2)

    @pl.when(ki == 0)
    def _():
        _init_state(m_scr, l_scr, acc_scr)

    delta = qi * tq - ki * tk

    @pl.when(delta + tq - 1 >= 0)
    def _():
        rel = lax.broadcasted_iota(jnp.int32, (tq, tk), 1) - lax.broadcasted_iota(jnp.int32, (tq, tk), 0)
        mask = rel <= delta
        qf = q_ref[0].astype(F32)
        k, v = k_ref[0], v_ref[0]
        fq, fk = fq_ref[0], fk_ref[0]
        for p in range(N_PAIRS):
            sl = slice(p * LANES, (p + 1) * LANES)
            idx = (2 * p, 2 * p + 1)
            state = _fox_pair(qf[:, sl], k[:, sl], v[:, sl],
                              [fq[:, h:h + 1] for h in idx], [fk[h:h + 1, :] for h in idx],
                              mask, _load_state(m_scr, l_scr, acc_scr, idx))
            _store_state(m_scr, l_scr, acc_scr, idx, state)

    @pl.when(ki == pl.num_programs(2) - 1)
    def _():
        lo = _lower_half((tq, LANES))
        for p in range(N_PAIRS):
            out = _fox_out(_load_state(m_scr, l_scr, acc_scr, (2 * p, 2 * p + 1)), lo)
            o_ref[0, :, p * LANES:(p + 1) * LANES] = out.astype(o_ref.dtype)


def _diff_kernel(q_ref, k_ref, v_ref, lam_ref, g_ref, o_ref, m_scr, l_scr, acc_scr, *, lam_init):
    tq, tk = q_ref.shape[1], k_ref.shape[1]
    qi, ki = pl.program_id(1), pl.program_id(2)

    @pl.when(ki == 0)
    def _():
        _init_state(m_scr, l_scr, acc_scr)

    delta = qi * (tq // CHUNK) - ki * (tk // CHUNK)

    @pl.when(delta * CHUNK + tq - 1 >= 0)
    def _():
        shift = _log2(CHUNK)
        rel = ((lax.broadcasted_iota(jnp.int32, (tq, tk), 1) >> shift)
               - (lax.broadcasted_iota(jnp.int32, (tq, tk), 0) >> shift))
        mask = rel <= delta
        qf = q_ref[0].astype(F32)
        k, v = k_ref[0], v_ref[0]
        for p in range(N_PAIRS):
            sl = slice(p * LANES, (p + 1) * LANES)
            idx = tuple(range(4 * p, 4 * p + 4))
            state = _diff_pair(qf[:, sl], k[:, sl], v[:, sl], mask,
                               _load_state(m_scr, l_scr, acc_scr, idx))
            _store_state(m_scr, l_scr, acc_scr, idx, state)

    @pl.when(ki == pl.num_programs(2) - 1)
    def _():
        lo = _lower_half((tq, LANES))
        lam = _lam(lam_ref, lam_init)
        for p in range(N_PAIRS):
            sl = slice(p * LANES, (p + 1) * LANES)
            state = _load_state(m_scr, l_scr, acc_scr, tuple(range(4 * p, 4 * p + 4)))
            o_ref[0, :, sl] = _diff_out(state, lo, lam, g_ref[:, sl], lam_init).astype(o_ref.dtype)


def _att_specs(b, l, tq, tk):
    assert tq % CHUNK == 0 and tk % CHUNK == 0 and l % tq == 0 and l % tk == 0
    last_k = lambda qi: (qi * tq + tq - 1) // tk
    qspec = pl.BlockSpec((1, tq, GROUP_W), lambda i, qi, ki: (i, qi, 0))
    kspec = pl.BlockSpec((1, tk, GROUP_W), lambda i, qi, ki: (i, jnp.minimum(ki, last_k(qi)), 0))
    return (b, l // tq, l // tk), qspec, kspec, last_k


def _fox_prompt(q, k, v, fq, fk_t):
    b, l, _ = q.shape
    tq = tk = min(ATT_TILE, l)
    grid, qspec, kspec, last_k = _att_specs(b, l, tq, tk)
    return pl.pallas_call(
        _fox_kernel, grid=grid,
        in_specs=[qspec, kspec, kspec,
                  pl.BlockSpec((1, tq, fq.shape[-1]), lambda i, qi, ki: (i, qi, 0)),
                  pl.BlockSpec((1, F_ROWS, tk), lambda i, qi, ki: (i, 0, jnp.minimum(ki, last_k(qi))))],
        out_specs=qspec,
        out_shape=jax.ShapeDtypeStruct(q.shape, BF16),
        scratch_shapes=[pltpu.VMEM((N_HEADS, tq, 1), F32), pltpu.VMEM((N_HEADS, tq, 1), F32),
                        pltpu.VMEM((N_HEADS, tq, LANES), F32)],
        compiler_params=_params("arbitrary", "arbitrary", "arbitrary"),
        name="fox_attention",
    )(q, k, v, fq, fk_t)


def _diff_prompt(q, k, v, lam_rows, gain, lam_init):
    b, l, _ = q.shape
    tq = tk = min(ATT_TILE, l)
    grid, qspec, kspec, _ = _att_specs(b, l, tq, tk)
    n = 2 * N_HEADS
    return pl.pallas_call(
        functools.partial(_diff_kernel, lam_init=lam_init), grid=grid,
        in_specs=[qspec, kspec, kspec, _resident(lam_rows.shape), _resident((1, GROUP_W))],
        out_specs=qspec,
        out_shape=jax.ShapeDtypeStruct(q.shape, BF16),
        scratch_shapes=[pltpu.VMEM((n, tq, 1), F32), pltpu.VMEM((n, tq, 1), F32),
                        pltpu.VMEM((n, tq, LANES), F32)],
        compiler_params=_params("arbitrary", "arbitrary", "arbitrary"),
        name="diff_attention",
    )(q, k, v, lam_rows, gain)


def _fox_sample_kernel(q_ref, kp_ref, vp_ref, kn_ref, vn_ref, fq_ref, fkp_ref, fkn_ref, o_ref):
    tq, tn = q_ref.shape[1], kn_ref.shape[1]
    qf = q_ref[0].astype(F32)
    kp, vp = kp_ref[0].astype(BF16), vp_ref[0].astype(BF16)
    kn, vn = kn_ref[0], vn_ref[0]
    fq, fkp, fkn = fq_ref[0], fkp_ref[0], fkn_ref[0]
    causal = lax.broadcasted_iota(jnp.int32, (tq, tn), 1) <= lax.broadcasted_iota(jnp.int32, (tq, tn), 0)
    lo = _lower_half((tq, LANES))
    for p in range(N_PAIRS):
        sl = slice(p * LANES, (p + 1) * LANES)
        idx = (2 * p, 2 * p + 1)
        fq_cols = [fq[:, h:h + 1] for h in idx]
        state = _fox_pair(qf[:, sl], kp[:, sl], vp[:, sl], fq_cols,
                          [fkp[h:h + 1, :] for h in idx], None, _fresh(2, tq))
        state = _fox_pair(qf[:, sl], kn[:, sl], vn[:, sl], fq_cols,
                          [fkn[h:h + 1, :] for h in idx], causal, state)
        o_ref[0, :, sl] = _fox_out(state, lo).astype(o_ref.dtype)


def _diff_sample_kernel(q_ref, kp_ref, vp_ref, kn_ref, vn_ref, lam_ref, g_ref, o_ref,
                        *, lam_init, past_len, n_new):
    tq, tn = q_ref.shape[1], kn_ref.shape[1]
    qf = q_ref[0].astype(F32)
    kp, vp = kp_ref[0].astype(BF16), vp_ref[0].astype(BF16)
    kn, vn = kn_ref[0], vn_ref[0]
    col = lax.broadcasted_iota(jnp.int32, (tq, tn), 1)
    row = lax.broadcasted_iota(jnp.int32, (tq, tn), 0)
    shift = _log2(CHUNK)
    visible = (((past_len + col) >> shift) <= ((past_len + row) >> shift)) & (col < n_new)
    lo = _lower_half((tq, LANES))
    lam = _lam(lam_ref, lam_init)
    for p in range(N_PAIRS):
        sl = slice(p * LANES, (p + 1) * LANES)
        state = _diff_pair(qf[:, sl], kp[:, sl], vp[:, sl], None, _fresh(4, tq))
        state = _diff_pair(qf[:, sl], kn[:, sl], vn[:, sl], visible, state)
        o_ref[0, :, sl] = _diff_out(state, lo, lam, g_ref[:, sl], lam_init).astype(o_ref.dtype)


def _sample_specs(q, k_past, k_new):
    b, lq, _ = q.shape
    seq = lambda rows: pl.BlockSpec((1, rows, GROUP_W), lambda i: (i, 0, 0))
    return b, seq(lq), seq(k_past.shape[1]), seq(k_new.shape[1])


def _fox_sample(q, k_past, v_past, k_new, v_new, fq, fk_past, fk_new):
    b, qspec, pspec, nspec = _sample_specs(q, k_past, k_new)
    frow = lambda a: pl.BlockSpec((1,) + a.shape[1:], lambda i: (i, 0, 0))
    return pl.pallas_call(
        _fox_sample_kernel, grid=(b,),
        in_specs=[qspec, pspec, pspec, nspec, nspec, frow(fq), frow(fk_past), frow(fk_new)],
        out_specs=qspec,
        out_shape=jax.ShapeDtypeStruct(q.shape, BF16),
        compiler_params=_params("arbitrary"),
        name="fox_attention_cached",
    )(q, k_past, v_past, k_new, v_new, fq, fk_past, fk_new)


def _diff_sample(q, k_past, v_past, k_new, v_new, lam_rows, gain, lam_init, n_new):
    b, qspec, pspec, nspec = _sample_specs(q, k_past, k_new)
    past_len = k_past.shape[1]
    assert past_len % CHUNK == 0
    return pl.pallas_call(
        functools.partial(_diff_sample_kernel, lam_init=lam_init, past_len=past_len, n_new=n_new),
        grid=(b,),
        in_specs=[qspec, pspec, pspec, nspec, nspec, _resident(lam_rows.shape), _resident((1, GROUP_W))],
        out_specs=qspec,
        out_shape=jax.ShapeDtypeStruct(q.shape, BF16),
        compiler_params=_params("arbitrary"),
        name="diff_attention_cached",
    )(q, k_past, v_past, k_new, v_new, lam_rows, gain)


def _mlp_kernel(x_ref, r_ref, f_ref, d_ref, g1_ref, sc_ref, sh_ref, g2_ref, ng_ref,
                wo_ref, wu_ref, wd_ref, fg_ref, o_ref, *, final_norm):
    bt, tl, d = x_ref.shape
    m = bt * tl
    mix = jnp.zeros((m, d), F32)
    for i, ref in enumerate((r_ref, f_ref, d_ref)):
        mix = mix + jnp.dot(ref[...].reshape(m, GROUP_W), wo_ref[i], preferred_element_type=F32)
    x = x_ref[...] + g1_ref[...] * mix.reshape(bt, tl, d)
    hb = _rms_mod(x, ng_ref[...], sc_ref[...], sh_ref[...]).reshape(m, d).astype(BF16)
    d_ff = wu_ref.shape[1]
    step = d_ff // 4
    y = jnp.zeros((m, d), F32)
    for j in range(0, d_ff, step):
        u = jnp.maximum(jnp.dot(hb, wu_ref[:, j:j + step], preferred_element_type=F32), 0.0)
        y = y + jnp.dot((u * u).astype(BF16), wd_ref[j:j + step, :], preferred_element_type=F32)
    x = x + g2_ref[...] * y.reshape(bt, tl, d)
    if final_norm:
        x = x * lax.rsqrt(jnp.mean(x * x, axis=-1, keepdims=True) + EPS) * fg_ref[...]
    o_ref[...] = x


def _mlp(x, r, f, dd, g1, sc2, sh2, g2, norm_g, w_out3, w_up, w_down, final_g, bt, tl, final_norm):
    b, l, d = x.shape
    tile = lambda w: pl.BlockSpec((bt, tl, w), lambda i, t: (i, t, 0))
    mod = pl.BlockSpec((bt, 1, d), lambda i, t: (i, 0, 0))
    return pl.pallas_call(
        functools.partial(_mlp_kernel, final_norm=final_norm),
        grid=(b // bt, l // tl),
        in_specs=[tile(d), tile(GROUP_W), tile(GROUP_W), tile(GROUP_W), mod, mod, mod, mod,
                  _resident((1, d)), _resident(w_out3.shape), _resident(w_up.shape),
                  _resident(w_down.shape), _resident((1, d))],
        out_specs=tile(d),
        out_shape=jax.ShapeDtypeStruct(x.shape, F32),
        compiler_params=_params("arbitrary", "arbitrary"),
        name="out_proj_mlp",
    )(x, r, f, dd, g1, sc2, sh2, g2, norm_g, w_out3, w_up, w_down, final_g)


def _rope_tables(pos, half, batch_reps):
    inv = ROPE_THETA ** (-jnp.arange(half, dtype=F32) / half)
    ang = pos.astype(F32)[:, None] * inv[None, :]
    cos, sin = jnp.cos(ang), jnp.sin(ang)
    reps = GROUP_W // (2 * half)
    cos = jnp.tile(jnp.concatenate([cos, cos], axis=-1), (batch_reps, reps))
    sin = jnp.tile(jnp.concatenate([-sin, sin], axis=-1), (batch_reps, reps))
    return cos, sin


def _pad_axis(a, axis, size):
    pad = [(0, 0)] * a.ndim
    pad[axis] = (0, size - a.shape[axis])
    return jnp.pad(a, pad)


def _forget_cumsum(logf_all):
    lk = logf_all.shape[1]
    x = jnp.transpose(logf_all, (0, 2, 1))
    x = _pad_axis(_pad_axis(x, 1, F_ROWS), 2, -(-lk // LANES) * LANES)
    return _cumsum(x)


def _group_layer(x, mods, tabs, past, lw, lam_init, bt, tl, final_g, final_norm):
    b, l, d = x.shape
    sh1, sc1, g1, sh2, sc2, g2 = mods
    (rq, rk, rv, rg, fq, fk_f, fk_b, fv_f, fv_b, logf_pad,
     dq, dk_f, dk_b, dv_f, dv_b) = _inproj(x, sc1, sh1, lw["norm1_g"], lw["w_main"], lw["w_ff"],
                                           lw["b_ff"], tabs, bt, tl)
    logf = logf_pad[:, :, :N_HEADS]
    if past is None:
        s0 = jnp.zeros((b, N_HEADS, HEAD_DIM, HEAD_DIM), F32)
        f_t = _forget_cumsum(logf)
        fq_cum = jnp.transpose(f_t[:, :N_HEADS + 2, :l], (0, 2, 1))
        f_out = _fox_prompt(fq, fk_b, fv_b, fq_cum, f_t)
        d_out = _diff_prompt(dq, dk_b, dv_b, lw["lam_rows"], lw["diff_norm_g"], lam_init)
    else:
        pfk, pfv, plogf, pdk, pdv, s0 = past
        past_len = pfk.shape[1]
        f_t = _forget_cumsum(jnp.concatenate([plogf, logf], axis=1))
        fq_cum = jnp.transpose(f_t[:, :N_HEADS + 2, past_len:past_len + l], (0, 2, 1))
        new_rows = lambda a: _pad_axis(a, 1, LANES)
        f_out = _fox_sample(fq, pfk.reshape(b, past_len, GROUP_W), pfv.reshape(b, past_len, GROUP_W),
                            new_rows(fk_b), new_rows(fv_b), fq_cum,
                            f_t[:, :, :past_len], f_t[:, :, past_len:past_len + LANES])
        d_out = _diff_sample(dq, pdk.reshape(b, past_len, GROUP_W), pdv.reshape(b, past_len, GROUP_W),
                             new_rows(dk_b), new_rows(dv_b), lw["lam_rows"], lw["diff_norm_g"],
                             lam_init, l)
    r_out, s_pairs = _retention(rq, rk, rv, rg, _state_to_pairs(s0), lw["ret_norm_g"],
                                lw["ret_norm_b"], bt, tl, min(CHUNK, l))
    x = _mlp(x, r_out, f_out, d_out, g1, sc2, sh2, g2, lw["norm2_g"], lw["w_out3"], lw["w_up"],
             lw["w_down"], final_g, bt, tl, final_norm)
    heads = lambda a: a.reshape(b, l, N_HEADS, HEAD_DIM)
    return x, (heads(fk_f), heads(fv_f), logf, heads(dk_f), heads(dv_f), _pairs_to_state(s_pairs))


def kernel(x_prompt, x_sample, cache_fox_k, cache_fox_v, cache_fox_logf, cache_diff_k, cache_diff_v,
           state_ret, c_prompt, c_sample, norm1_g, norm2_g, w_ada, b_ada, w_in, b_forget,
           ret_norm_g, ret_norm_b, lam_q1, lam_k1, lam_q2, lam_k2, diff_norm_g, w_out, w_up, w_down,
           final_g):
    depth, d = norm1_g.shape
    bp, lp, _ = x_prompt.shape
    bs, ls, _ = x_sample.shape
    past_len = cache_fox_k.shape[2]

    rows = -(-(bp + bs) // 8) * 8
    c_all = _pad_axis(jnp.concatenate([c_prompt, c_sample], axis=0), 0, rows)
    mod = _ada(c_all, w_ada, b_ada)

    def mods(li, r0, n):
        return [mod[li, r0:r0 + n, i * d:(i + 1) * d].reshape(n, 1, d) for i in range(6)]

    ff0 = 7 * GROUP_W
    w_main = jnp.concatenate([w_in[:, :, :ff0], w_in[:, :, ff0 + N_HEADS:]], axis=-1).astype(BF16)
    w_ff = _pad_axis(w_in[:, :, ff0:ff0 + N_HEADS], 2, LANES).astype(BF16)
    b_ff = _pad_axis(b_forget, 1, LANES).reshape(depth, 1, LANES)
    w_out3 = w_out.reshape(depth, 3, GROUP_W, d).astype(BF16)
    w_up_b, w_down_b = w_up.astype(BF16), w_down.astype(BF16)
    lam_rows = _pad_axis(_pad_axis(jnp.stack([lam_q1, lam_k1, lam_q2, lam_k2], axis=1), 2, LANES), 1, 8)
    final_row = final_g.reshape(1, d)

    tl_p = min(TOK_TILE, lp)
    p_pos = jnp.arange(lp, dtype=jnp.int32)
    s_pos = past_len + jnp.arange(ls, dtype=jnp.int32)
    tabs_p = _rope_tables(p_pos, HEAD_DIM // 2, 1) + _rope_tables(p_pos, DIFF_QK // 2, 1)
    tabs_s = _rope_tables(s_pos, HEAD_DIM // 2, bs) + _rope_tables(s_pos, DIFF_QK // 2, bs)

    xp, xs = x_prompt, x_sample
    p_new, s_new = [], []
    for li in range(depth):
        lam_init = 0.8 - 0.6 * math.exp(-0.3 * li)
        lw = dict(norm1_g=norm1_g[li].reshape(1, d), norm2_g=norm2_g[li].reshape(1, d),
                  w_main=w_main[li], w_ff=w_ff[li], b_ff=b_ff[li],
                  ret_norm_g=ret_norm_g[li].reshape(1, GROUP_W), ret_norm_b=ret_norm_b[li].reshape(1, GROUP_W),
                  lam_rows=lam_rows[li], diff_norm_g=diff_norm_g[li].reshape(1, GROUP_W),
                  w_out3=w_out3[li], w_up=w_up_b[li], w_down=w_down_b[li])
        last = li == depth - 1
        xp, ps = _group_layer(xp, mods(li, 0, bp), tabs_p, None, lw, lam_init, 1, tl_p, final_row, last)
        past = (cache_fox_k[li], cache_fox_v[li], cache_fox_logf[li], cache_diff_k[li], cache_diff_v[li],
                state_ret[li])
        xs, ss = _group_layer(xs, mods(li, bp, bs), tabs_s, past, lw, lam_init, bs, ls, final_row, last)
        p_new.append(ps)
        s_new.append(ss)

    stack = lambda new, i: jnp.stack([s[i] for s in new])
    return ((xp, xs) + tuple(stack(p_new, i) for i in range(6)) + tuple(stack(s_new, i) for i in range(6)))
```

```python
import functools
import math

import jax
import jax.numpy as jnp
from jax import lax
from jax.experimental import pallas as pl
from jax.experimental.pallas import tpu as pltpu

F32 = jnp.float32
BF16 = jnp.bfloat16

CHUNK = 64
HEAD_DIM = 64
N_HEADS = 6
GROUP_W = N_HEADS * HEAD_DIM
LANES = 128
N_PAIRS = GROUP_W // LANES
DIFF_QK = HEAD_DIM // 2
ROPE_THETA = 10000.0
EPS = 1e-6
MASKED = -1e30
LOG2E = math.log2(math.e)
STRIP = 16
GROUP_ROWS = 256
LOG_G = tuple(math.log1p(-(2.0 ** (-5.0 - h))) for h in range(N_HEADS))
TOK_TILE = 512
ATT_TILE = 512
F_ROWS = 16
VMEM_LIMIT = 56 * 1024 * 1024


def _params(*sem):
    return pltpu.CompilerParams(dimension_semantics=sem, vmem_limit_bytes=VMEM_LIMIT)


def _resident(shape):
    zeros = (0,) * len(shape)
    return pl.BlockSpec(shape, lambda *_: zeros, pipeline_mode=pl.Buffered(1))


def _sigmoid(x):
    return 1.0 / (1.0 + jnp.exp(-x))


def _log2(n):
    assert n > 0 and n & (n - 1) == 0, n
    return n.bit_length() - 1


def _dot_nt(a, b):
    return lax.dot_general(a, b, (((1,), (1,)), ((), ())), preferred_element_type=F32)


def _dot_tn(a, b):
    return lax.dot_general(a, b, (((0,), (0,)), ((), ())), preferred_element_type=F32)


def _rms_mod(x, g, scale, shift):
    y = x * lax.rsqrt(jnp.mean(x * x, axis=-1, keepdims=True) + EPS)
    return (y * g) * (1.0 + scale) + shift


def _lower_half(shape):
    return lax.broadcasted_iota(jnp.int32, shape, len(shape) - 1) < HEAD_DIM


def _half_sums(x, lo):
    a = jnp.sum(jnp.where(lo, x, 0.0), axis=-1, keepdims=True)
    b = jnp.sum(jnp.where(lo, 0.0, x), axis=-1, keepdims=True)
    return jnp.where(lo, a, b)


def _ada_kernel(c_ref, w_ref, b_ref, o_ref):
    c = c_ref[...]
    a = (c * _sigmoid(c)).astype(BF16)
    o_ref[0] = jnp.dot(a, w_ref[0].astype(BF16), preferred_element_type=F32) + b_ref[0]


def _ada(c_all, w_ada, b_ada):
    depth, d, n = w_ada.shape
    rows = c_all.shape[0]
    tn = n // 4
    return pl.pallas_call(
        _ada_kernel,
        grid=(depth, n // tn),
        in_specs=[
            pl.BlockSpec((rows, d), lambda l, j: (0, 0)),
            pl.BlockSpec((1, d, tn), lambda l, j: (l, 0, j)),
            pl.BlockSpec((1, 1, tn), lambda l, j: (l, 0, j)),
        ],
        out_specs=pl.BlockSpec((1, rows, tn), lambda l, j: (l, 0, j)),
        out_shape=jax.ShapeDtypeStruct((depth, rows, n), F32),
        compiler_params=_params("arbitrary", "arbitrary"),
        name="ada_mod",
    )(c_all, w_ada, b_ada.reshape(depth, 1, n))


def _rope(y, cos, sin_signed, half):
    width = y.shape[-1]
    lane = lax.broadcasted_iota(jnp.int32, y.shape, 1)
    ahead = pltpu.roll(y, width - half, axis=1)
    behind = pltpu.roll(y, half, axis=1)
    swapped = jnp.where((lane & half) == 0, ahead, behind)
    return y * cos + swapped * sin_signed


def _inproj_kernel(x_ref, sc_ref, sh_ref, g_ref, wm_ref, wf_ref, bf_ref,
                   c64_ref, s64_ref, c32_ref, s32_ref,
                   rq_o, rk_o, rv_o, rg_o, fq_o, fkf_o, fkb_o, fvf_o, fvb_o, lf_o,
                   dq_o, dkf_o, dkb_o, dvf_o, dvb_o):
    bt, tl, d = x_ref.shape
    m = bt * tl
    h = _rms_mod(x_ref[...], g_ref[...], sc_ref[...], sh_ref[...])
    hb = h.reshape(m, d).astype(BF16)

    def piece(i):
        return jnp.dot(hb, wm_ref[:, i * GROUP_W:(i + 1) * GROUP_W], preferred_element_type=F32)

    def put(y, *refs):
        for ref in refs:
            if len(ref.shape) == 4:
                for p in range(N_PAIRS):
                    yp = y[:, p * LANES:(p + 1) * LANES]
                    ref[:, p] = yp.reshape(bt, tl, LANES).astype(ref.dtype)
            else:
                ref[...] = y.reshape(bt, tl, y.shape[-1]).astype(ref.dtype)

    c64, s64 = c64_ref[...], s64_ref[...]
    c32, s32 = c32_ref[...], s32_ref[...]
    put(_rope(piece(0), c64, s64, HEAD_DIM // 2), rq_o)
    put(_rope(piece(1), c64, s64, HEAD_DIM // 2), rk_o)
    put(piece(2), rv_o)
    put(piece(3), rg_o)
    put(piece(4) * (HEAD_DIM ** -0.5 * LOG2E), fq_o)
    put(piece(5), fkf_o, fkb_o)
    put(piece(6), fvf_o, fvb_o)
    z = jnp.dot(hb, wf_ref[...], preferred_element_type=F32) + bf_ref[...]
    put(jnp.minimum(z, 0.0) - jnp.log1p(jnp.exp(-jnp.abs(z))), lf_o)
    put(_rope(piece(7), c32, s32, DIFF_QK // 2) * (DIFF_QK ** -0.5 * LOG2E), dq_o)
    put(_rope(piece(8), c32, s32, DIFF_QK // 2), dkf_o, dkb_o)
    put(piece(9), dvf_o, dvb_o)


def _inproj(x, scale, shift, gain, w_main, w_ff, b_ff, tabs, bt, tl):
    b, l, d = x.shape
    m = bt * tl
    grid = (b // bt, l // tl)
    tile = lambda w: pl.BlockSpec((bt, tl, w), lambda i, t: (i, t, 0))
    mod = pl.BlockSpec((bt, 1, d), lambda i, t: (i, 0, 0))
    tab = pl.BlockSpec((m, GROUP_W), lambda i, t: (t, 0))
    pairs = (pl.BlockSpec((bt, N_PAIRS, tl, LANES), lambda i, t: (i, 0, t, 0)),
             jax.ShapeDtypeStruct((b, N_PAIRS, l, LANES), BF16))
    flat = lambda w, dt: (tile(w), jax.ShapeDtypeStruct((b, l, w), dt))
    outs = [
        flat(GROUP_W, BF16), flat(GROUP_W, BF16), flat(GROUP_W, BF16), flat(GROUP_W, F32),
        pairs, flat(GROUP_W, F32), pairs, flat(GROUP_W, F32), pairs,
        flat(LANES, F32),
        pairs, flat(GROUP_W, F32), pairs, flat(GROUP_W, F32), pairs,
    ]
    return pl.pallas_call(
        _inproj_kernel,
        grid=grid,
        in_specs=[tile(d), mod, mod, _resident((1, d)), _resident(w_main.shape),
                  _resident(w_ff.shape), _resident((1, LANES)), tab, tab, tab, tab],
        out_specs=[spec for spec, _ in outs],
        out_shape=[shape for _, shape in outs],
        compiler_params=_params("arbitrary", "arbitrary"),
        name="in_proj",
    )(x, scale, shift, gain, w_main, w_ff, b_ff, *tabs)


def _cumsum_kernel(x_ref, o_ref):
    rows, n = x_ref.shape[1], x_ref.shape[2]
    r = lax.broadcasted_iota(jnp.int32, (LANES, LANES), 0)
    c = lax.broadcasted_iota(jnp.int32, (LANES, LANES), 1)
    upper = (r <= c).astype(BF16)
    carry = jnp.zeros((rows, 1), F32)
    for j in range(n // LANES):
        x = x_ref[0, :, j * LANES:(j + 1) * LANES]
        hi = x.astype(BF16)
        r1 = x - hi.astype(F32)
        mid = r1.astype(BF16)
        lo = (r1 - mid.astype(F32)).astype(BF16)
        y = jnp.dot(jnp.concatenate([hi, mid, lo], axis=0), upper, preferred_element_type=F32)
        y = y[:rows] + y[rows:2 * rows] + y[2 * rows:]
        o_ref[0, :, j * LANES:(j + 1) * LANES] = (y + carry) * LOG2E
        carry = carry + y[:, LANES - 1:LANES]


def _cumsum(logf_t):
    b, rows, n = logf_t.shape
    spec = pl.BlockSpec((1, rows, n), lambda i: (i, 0, 0))
    return pl.pallas_call(
        _cumsum_kernel, grid=(b,), in_specs=[spec], out_specs=spec,
        out_shape=jax.ShapeDtypeStruct(logf_t.shape, F32),
        compiler_params=_params("arbitrary"),
        name="forget_cumsum",
    )(logf_t)


def _ret_kernel(q_ref, k_ref, v_ref, g_ref, s0_ref, ng_ref, nb_ref, o_ref, sn_ref, w_scr, s_scr,
                *, chunk):
    bt, tl, _ = q_ref.shape
    m = bt * tl
    t = pl.program_id(1)

    @pl.when((pl.program_id(0) == 0) & (t == 0))
    def _():
        i = lax.broadcasted_iota(jnp.int32, (m, m), 0)
        j = lax.broadcasted_iota(jnp.int32, (m, m), 1)
        visible = ((i >> _log2(tl)) == (j >> _log2(tl))) & ((j >> _log2(chunk)) <= (i >> _log2(chunk)))
        dist = jnp.abs(i - j).astype(F32)
        for h in range(N_HEADS):
            w_scr[h] = jnp.where(visible, jnp.exp(LOG_G[h] * dist), 0.0)

    @pl.when(t == 0)
    def _():
        s_scr[...] = s0_ref[...]

    head = lax.broadcasted_iota(jnp.int32, (1, GROUP_W), 1) >> _log2(HEAD_DIM)
    log_g = jnp.full((1, GROUP_W), LOG_G[N_HEADS - 1], F32)
    for h in range(N_HEADS - 1):
        log_g = jnp.where(head == h, LOG_G[h], log_g)
    pos = (lax.broadcasted_iota(jnp.int32, (m, 1), 0) & (tl - 1)).astype(F32)
    q_w = jnp.exp((pos + 1.0) * log_g)
    k_w = jnp.exp((tl - 1.0 - pos) * log_g) * HEAD_DIM ** -0.5

    qf = q_ref[...].reshape(m, GROUP_W).astype(F32)
    kf = k_ref[...].reshape(m, GROUP_W).astype(F32)
    v = v_ref[...].reshape(m, GROUP_W)
    gate = g_ref[...].reshape(m, GROUP_W)
    qwb = (qf * q_w).astype(BF16)
    kwb = (kf * k_w).astype(BF16)
    ksb = (kf * HEAD_DIM ** -0.5).astype(BF16)
    lo = _lower_half((m, LANES))
    r = lax.broadcasted_iota(jnp.int32, (LANES, LANES), 0) < HEAD_DIM
    c = lax.broadcasted_iota(jnp.int32, (LANES, LANES), 1) < HEAD_DIM
    same_head = r == c

    for p in range(N_PAIRS):
        sl = slice(p * LANES, (p + 1) * LANES)
        qp, vp = qf[:, sl], v[:, sl]
        q2 = jnp.concatenate([jnp.where(lo, qp, 0.0), jnp.where(lo, 0.0, qp)], axis=0).astype(BF16)
        s = _dot_nt(q2, ksb[:, sl])
        probs = jnp.concatenate([(s[:m] * w_scr[2 * p]).astype(BF16),
                                 (s[m:] * w_scr[2 * p + 1]).astype(BF16)], axis=0)
        o2 = jnp.dot(probs, vp, preferred_element_type=F32)
        intra = jnp.where(lo, o2[:m], o2[m:])
        tile_decay = jnp.where(r, math.exp(tl * LOG_G[2 * p]), math.exp(tl * LOG_G[2 * p + 1]))
        cross = []
        for bi in range(bt):
            rows = slice(bi * tl, (bi + 1) * tl)
            state = s_scr[bi, p]
            cross.append(jnp.dot(qwb[rows, sl], state.astype(BF16), preferred_element_type=F32))
            u = _dot_tn(kwb[rows, sl], vp[rows])
            s_scr[bi, p] = tile_decay * state + jnp.where(same_head, u, 0.0)
        y = intra + (jnp.concatenate(cross, axis=0) if bt > 1 else cross[0])
        yc = y - _half_sums(y, lo) * (1.0 / HEAD_DIM)
        var = _half_sums(yc * yc, lo) * (1.0 / HEAD_DIM)
        yn = yc * lax.rsqrt(var + EPS) * ng_ref[:, sl] + nb_ref[:, sl]
        gp = gate[:, sl]
        out = gp * _sigmoid(gp) * yn
        o_ref[:, :, sl] = out.reshape(bt, tl, LANES).astype(o_ref.dtype)

    @pl.when(t == pl.num_programs(1) - 1)
    def _():
        sn_ref[...] = s_scr[...]


def _retention(rq, rk, rv, rg, s0_pairs, norm_g, norm_b, bt, tl, chunk):
    b, l, _ = rq.shape
    m = bt * tl
    tile = pl.BlockSpec((bt, tl, GROUP_W), lambda i, t: (i, t, 0))
    st = pl.BlockSpec((bt, N_PAIRS, LANES, LANES), lambda i, t: (i, 0, 0, 0))
    return pl.pallas_call(
        functools.partial(_ret_kernel, chunk=chunk),
        grid=(b // bt, l // tl),
        in_specs=[tile, tile, tile, tile, st, _resident((1, GROUP_W)), _resident((1, GROUP_W))],
        out_specs=[tile, st],
        out_shape=[jax.ShapeDtypeStruct((b, l, GROUP_W), BF16),
                   jax.ShapeDtypeStruct(s0_pairs.shape, F32)],
        scratch_shapes=[pltpu.VMEM((N_HEADS, m, m), F32),
                        pltpu.VMEM((bt, N_PAIRS, LANES, LANES), F32)],
        compiler_params=_params("arbitrary", "arbitrary"),
        name="retention",
    )(rq, rk, rv, rg, s0_pairs, norm_g, norm_b)


def _state_to_pairs(s):
    b = s.shape[0]
    s = s.reshape(b, N_PAIRS, 2, HEAD_DIM, HEAD_DIM)
    z = jnp.zeros_like(s[:, :, 0])
    top = jnp.concatenate([s[:, :, 0], z], axis=-1)
    bot = jnp.concatenate([z, s[:, :, 1]], axis=-1)
    return jnp.concatenate([top, bot], axis=-2)


def _pairs_to_state(sp):
    b = sp.shape[0]
    a = sp[:, :, :HEAD_DIM, :HEAD_DIM]
    d = sp[:, :, HEAD_DIM:, HEAD_DIM:]
    return jnp.stack([a, d], axis=2).reshape(b, N_HEADS, HEAD_DIM, HEAD_DIM)


def _softmax_step(s, m, l):
    m_new = jnp.maximum(m, jnp.max(s, axis=-1, keepdims=True))
    alpha = jnp.exp2(m - m_new)
    p = jnp.exp2(s - m_new)
    return p, m_new, alpha * l + jnp.sum(p, axis=-1, keepdims=True), alpha


def _stack_masked(qp, n_streams):
    width = LANES // n_streams
    owner = lax.broadcasted_iota(jnp.int32, qp.shape, 1) >> _log2(width)
    return jnp.concatenate([jnp.where(owner == i, qp, 0.0) for i in range(n_streams)],
                           axis=0).astype(BF16)


def _fox_pair(qp, kp, vp, fq_cols, fk_rows, mask, state):
    tq = qp.shape[0]
    s = _dot_nt(_stack_masked(qp, 2), kp)
    probs, new = [], []
    for i in range(2):
        si = s[i * tq:(i + 1) * tq] + (fq_cols[i] - fk_rows[i])
        if mask is not None:
            si = jnp.where(mask, si, MASKED)
        p, m_new, l_new, alpha = _softmax_step(si, state[i][0], state[i][1])
        probs.append(p.astype(BF16))
        new.append((m_new, l_new, alpha))
    pv = jnp.dot(jnp.concatenate(probs, axis=0), vp, preferred_element_type=F32)
    return [(m_new, l_new, alpha * state[i][2] + pv[i * tq:(i + 1) * tq])
            for i, (m_new, l_new, alpha) in enumerate(new)]


def _diff_pair(qp, kp, vp, mask, state):
    tq = qp.shape[0]
    s = _dot_nt(_stack_masked(qp, 4), kp)
    probs, new = [], []
    for i in range(4):
        si = s[i * tq:(i + 1) * tq]
        if mask is not None:
            si = jnp.where(mask, si, MASKED)
        p, m_new, l_new, alpha = _softmax_step(si, state[i][0], state[i][1])
        probs.append(p.astype(BF16))
        new.append((m_new, l_new, alpha))
    pv = jnp.dot(jnp.concatenate(probs, axis=0), vp, preferred_element_type=F32)
    return [(m_new, l_new, alpha * state[i][2] + pv[i * tq:(i + 1) * tq])
            for i, (m_new, l_new, alpha) in enumerate(new)]


def _fresh(n, tq):
    return [(jnp.full((tq, 1), MASKED, F32), jnp.zeros((tq, 1), F32), jnp.zeros((tq, LANES), F32))
            for _ in range(n)]


def _fox_out(state, lo):
    (_, la, acca), (_, lb, accb) = state
    return jnp.where(lo, acca / la, accb / lb)


def _lam(lam_ref, lam_init):
    a = lam_ref[...]
    t1 = jnp.sum(a[0:1] * a[1:2], axis=-1, keepdims=True)
    t2 = jnp.sum(a[2:3] * a[3:4], axis=-1, keepdims=True)
    return jnp.exp(t1) - jnp.exp(t2) + lam_init


def _diff_norm(o, lo, gain, lam_init):
    ms = _half_sums(o * o, lo) * (1.0 / HEAD_DIM)
    return (o * lax.rsqrt(ms + EPS) * gain) * (1.0 - lam_init)


def _diff_out(state, lo, lam, gain, lam_init):
    (_, l0, a0), (_, l1, a1), (_, l2, a2), (_, l3, a3) = state
    o = jnp.where(lo, a0 / l0, a2 / l2) - lam * jnp.where(lo, a1 / l1, a3 / l3)
    return _diff_norm(o, lo, gain, lam_init)


def _rows(r0, n):
    return slice(r0, r0 + n)


def _tile_update(scr, q_scr, k, v, tq, key_bias_fn, row_bias_scr, visibility):
    s_scr, p_scr, m_scr, c_scr, a_scr, acc_scr = scr
    n_rows, tk = s_scr.shape
    n_chunks = tk // LANES
    chunk = lambda j: slice(j * LANES, (j + 1) * LANES)
    for r0 in range(0, n_rows, GROUP_ROWS):
        rows = _rows(r0, GROUP_ROWS)
        s_scr[rows, :] = _dot_nt(q_scr[rows, :], k)
    for r0 in range(0, n_rows, STRIP):
        rows = _rows(r0, STRIP)
        stream, row_in_stream = divmod(r0, tq)
        live = []
        for j in range(n_chunks):
            vis = visibility(row_in_stream, j)
            if vis is None:
                continue
            t = s_scr[rows, chunk(j)]
            if key_bias_fn is not None:
                t = t - key_bias_fn(stream, j)
            if vis is not True:
                t = jnp.where(vis, t, MASKED)
            if key_bias_fn is not None or vis is not True:
                s_scr[rows, chunk(j)] = t
            live.append(t)
        top = functools.reduce(jnp.maximum, live)
        top = jnp.broadcast_to(jnp.max(top, axis=-1, keepdims=True), (STRIP, LANES))
        m_old = m_scr[rows, :]
        if row_bias_scr is not None:
            row_bias = row_bias_scr[rows, :]
            m_new = jnp.maximum(m_old, top + row_bias)
            c_scr[rows, :] = m_new - row_bias
        else:
            m_new = jnp.maximum(m_old, top)
            c_scr[rows, :] = m_new
        a_scr[rows, :] = jnp.exp2(m_old - m_new)
        m_scr[rows, :] = m_new
    for r0 in range(0, n_rows, STRIP):
        rows = _rows(r0, STRIP)
        centre = c_scr[rows, :]
        for j in range(n_chunks):
            if visibility(r0 % tq, j) is None:
                p_scr[rows, chunk(j)] = jnp.zeros((STRIP, LANES), BF16)
            else:
                p_scr[rows, chunk(j)] = jnp.exp2((s_scr[rows, chunk(j)] - centre).astype(BF16))
    ones_col = (lax.broadcasted_iota(jnp.int32, v.shape, 1) == 0).astype(BF16)
    v_ext = jnp.concatenate([v, ones_col], axis=1)
    for r0 in range(0, n_rows, GROUP_ROWS):
        rows = _rows(r0, GROUP_ROWS)
        alpha = a_scr[rows, :]
        acc_scr[rows, :] = (jnp.concatenate([alpha, alpha], axis=1) * acc_scr[rows, :]
                            + jnp.dot(p_scr[rows, :], v_ext, preferred_element_type=F32))


def _all_visible(row0, j):
    return True


def _causal_visibility(row0, j):
    if j * LANES > row0 + STRIP - 1:
        return None
    if j * LANES + LANES - 1 <= row0:
        return True
    rel = (lax.broadcasted_iota(jnp.int32, (STRIP, LANES), 1)
           - lax.broadcasted_iota(jnp.int32, (STRIP, LANES), 0))
    return rel <= row0 - j * LANES


def _chunk_visibility(row0, j):
    q_chunk = row0 // CHUNK
    first, last = j * (LANES // CHUNK), (j + 1) * (LANES // CHUNK) - 1
    if first > q_chunk:
        return None
    if last <= q_chunk:
        return True
    return (lax.broadcasted_iota(jnp.int32, (STRIP, LANES), 1) >> _log2(CHUNK)) <= q_chunk - first


def _init_att_state(m_scr, acc_scr):
    m_scr[...] = jnp.full(m_scr.shape, MASKED, F32)
    acc_scr[...] = jnp.zeros(acc_scr.shape, F32)


def _normalised(acc_scr):
    acc = acc_scr[...]
    return acc[:, :LANES] / acc[:, LANES:LANES + 1]


def _fox_kernel(q_ref, k_ref, v_ref, fq_ref, fk_ref, o_ref,
                q_scr, fq_scr, m_scr, c_scr, a_scr, acc_scr, s_scr, p_scr):
    tq = q_ref.shape[2]
    tk = s_scr.shape[1]
    qi = pl.program_id(1)
    scr = (s_scr, p_scr, m_scr, c_scr, a_scr, acc_scr)
    chunks = tk // LANES
    head_col = lax.broadcasted_iota(jnp.int32, fq_ref.shape[1:], 1)
    lo = _lower_half((tq, LANES))

    def pair(p, carry):
        q_scr[...] = _stack_masked(q_ref[0, p].astype(F32), 2)
        for half in range(2):
            col = jnp.sum(jnp.where(head_col == 2 * p + half, fq_ref[0], 0.0), axis=-1, keepdims=True)
            fq_scr[half * tq:(half + 1) * tq, :] = jnp.broadcast_to(col, (tq, LANES))
        _init_att_state(m_scr, acc_scr)

        def tile(ki, visibility):
            keys = pl.ds(pl.multiple_of(ki * tk, tk), tk)
            key_bias = lambda stream, j: fk_ref[0, 2 * p + stream, ki * chunks + j]
            _tile_update(scr, q_scr, k_ref[0, p, keys, :], v_ref[0, p, keys, :], tq,
                         key_bias, fq_scr, visibility)

        def full_tile(ki, c):
            tile(ki, _all_visible)
            return c

        lax.fori_loop(0, qi, full_tile, 0)
        tile(qi, _causal_visibility)
        o = _normalised(acc_scr)
        o_ref[0, p] = jnp.where(lo, o[:tq], o[tq:]).astype(o_ref.dtype)
        return carry

    lax.fori_loop(0, N_PAIRS, pair, 0)


def _diff_kernel(q_ref, k_ref, v_ref, lam_ref, g_ref, o_ref,
                 q_scr, m_scr, c_scr, a_scr, acc_scr, s_scr, p_scr, *, lam_init):
    tq = q_ref.shape[2]
    tk = s_scr.shape[1]
    qi = pl.program_id(1)
    scr = (s_scr, p_scr, m_scr, c_scr, a_scr, acc_scr)
    lo = _lower_half((tq, LANES))
    lam = _lam(lam_ref, lam_init)

    def pair(p, carry):
        q_scr[...] = _stack_masked(q_ref[0, p].astype(F32), 4)
        _init_att_state(m_scr, acc_scr)

        def tile(ki, visibility):
            keys = pl.ds(pl.multiple_of(ki * tk, tk), tk)
            _tile_update(scr, q_scr, k_ref[0, p, keys, :], v_ref[0, p, keys, :], tq,
                         None, None, visibility)

        def full_tile(ki, c):
            tile(ki, _all_visible)
            return c

        lax.fori_loop(0, qi, full_tile, 0)
        tile(qi, _chunk_visibility)
        o = _normalised(acc_scr)
        o = (jnp.where(lo, o[:tq], o[2 * tq:3 * tq]) - lam * jnp.where(lo, o[tq:2 * tq], o[3 * tq:]))
        o_ref[0, p] = _diff_norm(o, lo, g_ref[p], lam_init).astype(o_ref.dtype)
        return carry

    lax.fori_loop(0, N_PAIRS, pair, 0)


def _prompt_att_call(kernel_fn, name, n_streams, q, k, v, extra, extra_specs, row_bias):
    b, _, l, _ = q.shape
    tq = tk = min(ATT_TILE, l)
    assert l % tq == 0 and tq % CHUNK == 0
    rows = n_streams * tq
    qspec = pl.BlockSpec((1, N_PAIRS, tq, LANES), lambda i, t: (i, 0, t, 0))
    kspec = pl.BlockSpec((1, N_PAIRS, l, LANES), lambda i, t: (i, 0, 0, 0), pipeline_mode=pl.Buffered(1))
    stats = pltpu.VMEM((rows, LANES), F32)
    return pl.pallas_call(
        kernel_fn, grid=(b, l // tq),
        in_specs=[qspec, kspec, kspec] + extra_specs,
        out_specs=qspec,
        out_shape=jax.ShapeDtypeStruct(q.shape, BF16),
        scratch_shapes=[pltpu.VMEM((rows, LANES), BF16)] + ([stats] if row_bias else [])
        + [stats, stats, stats, pltpu.VMEM((rows, 2 * LANES), F32),
           pltpu.VMEM((rows, tk), F32), pltpu.VMEM((rows, tk), BF16)],
        compiler_params=_params("arbitrary", "arbitrary"),
        name=name,
    )(q, k, v, *extra)


def _fox_prompt(q, k, v, fq, fk_rep):
    tq = min(ATT_TILE, q.shape[2])
    specs = [pl.BlockSpec((1, tq, fq.shape[-1]), lambda i, t: (i, t, 0)),
             pl.BlockSpec((1,) + fk_rep.shape[1:], lambda i, t: (i, 0, 0, 0, 0),
                          pipeline_mode=pl.Buffered(1))]
    return _prompt_att_call(_fox_kernel, "fox_attention", 2, q, k, v, (fq, fk_rep), specs, True)


def _diff_prompt(q, k, v, lam_rows, gain, lam_init):
    gain_pairs = gain.reshape(N_PAIRS, 1, LANES)
    specs = [_resident(lam_rows.shape), _resident(gain_pairs.shape)]
    return _prompt_att_call(functools.partial(_diff_kernel, lam_init=lam_init), "diff_attention", 4,
                            q, k, v, (lam_rows, gain_pairs), specs, False)


def _fox_sample_kernel(q_ref, kp_ref, vp_ref, kn_ref, vn_ref, fq_ref, fkp_ref, fkn_ref, o_ref):
    tq, tn = q_ref.shape[2], kn_ref.shape[2]
    kp, vp = kp_ref[0].astype(BF16), vp_ref[0].astype(BF16)
    fq, fkp, fkn = fq_ref[0], fkp_ref[0], fkn_ref[0]
    causal = lax.broadcasted_iota(jnp.int32, (tq, tn), 1) <= lax.broadcasted_iota(jnp.int32, (tq, tn), 0)
    lo = _lower_half((tq, LANES))
    for p in range(N_PAIRS):
        sl = slice(p * LANES, (p + 1) * LANES)
        idx = (2 * p, 2 * p + 1)
        fq_cols = [fq[:, h:h + 1] for h in idx]
        qf = q_ref[0, p].astype(F32)
        state = _fox_pair(qf, kp[:, sl], vp[:, sl], fq_cols,
                          [fkp[h:h + 1, :] for h in idx], None, _fresh(2, tq))
        state = _fox_pair(qf, kn_ref[0, p], vn_ref[0, p], fq_cols,
                          [fkn[h:h + 1, :] for h in idx], causal, state)
        o_ref[0, p] = _fox_out(state, lo).astype(o_ref.dtype)


def _diff_sample_kernel(q_ref, kp_ref, vp_ref, kn_ref, vn_ref, lam_ref, g_ref, o_ref,
                        *, lam_init, past_len, n_new):
    tq, tn = q_ref.shape[2], kn_ref.shape[2]
    kp, vp = kp_ref[0].astype(BF16), vp_ref[0].astype(BF16)
    col = lax.broadcasted_iota(jnp.int32, (tq, tn), 1)
    row = lax.broadcasted_iota(jnp.int32, (tq, tn), 0)
    shift = _log2(CHUNK)
    visible = (((past_len + col) >> shift) <= ((past_len + row) >> shift)) & (col < n_new)
    lo = _lower_half((tq, LANES))
    lam = _lam(lam_ref, lam_init)
    for p in range(N_PAIRS):
        sl = slice(p * LANES, (p + 1) * LANES)
        qf = q_ref[0, p].astype(F32)
        state = _diff_pair(qf, kp[:, sl], vp[:, sl], None, _fresh(4, tq))
        state = _diff_pair(qf, kn_ref[0, p], vn_ref[0, p], visible, state)
        o_ref[0, p] = _diff_out(state, lo, lam, g_ref[:, sl], lam_init).astype(o_ref.dtype)


def _sample_specs(q, k_past, k_new):
    b = q.shape[0]
    pairs = lambda a: pl.BlockSpec((1,) + a.shape[1:], lambda i: (i, 0, 0, 0))
    past = pl.BlockSpec((1, k_past.shape[1], GROUP_W), lambda i: (i, 0, 0))
    return b, pairs(q), past, pairs(k_new)


def _fox_sample(q, k_past, v_past, k_new, v_new, fq, fk_past, fk_new):
    b, qspec, pspec, nspec = _sample_specs(q, k_past, k_new)
    frow = lambda a: pl.BlockSpec((1,) + a.shape[1:], lambda i: (i, 0, 0))
    return pl.pallas_call(
        _fox_sample_kernel, grid=(b,),
        in_specs=[qspec, pspec, pspec, nspec, nspec, frow(fq), frow(fk_past), frow(fk_new)],
        out_specs=qspec,
        out_shape=jax.ShapeDtypeStruct(q.shape, BF16),
        compiler_params=_params("arbitrary"),
        name="fox_attention_cached",
    )(q, k_past, v_past, k_new, v_new, fq, fk_past, fk_new)


def _diff_sample(q, k_past, v_past, k_new, v_new, lam_rows, gain, lam_init, n_new):
    b, qspec, pspec, nspec = _sample_specs(q, k_past, k_new)
    past_len = k_past.shape[1]
    assert past_len % CHUNK == 0
    return pl.pallas_call(
        functools.partial(_diff_sample_kernel, lam_init=lam_init, past_len=past_len, n_new=n_new),
        grid=(b,),
        in_specs=[qspec, pspec, pspec, nspec, nspec, _resident(lam_rows.shape), _resident((1, GROUP_W))],
        out_specs=qspec,
        out_shape=jax.ShapeDtypeStruct(q.shape, BF16),
        compiler_params=_params("arbitrary"),
        name="diff_attention_cached",
    )(q, k_past, v_past, k_new, v_new, lam_rows, gain)


def _mlp_kernel(x_ref, r_ref, f_ref, d_ref, g1_ref, sc_ref, sh_ref, g2_ref, ng_ref,
                wo_ref, wu_ref, wd_ref, fg_ref, o_ref, *, final_norm):
    bt, tl, d = x_ref.shape
    m = bt * tl
    mix = jnp.dot(r_ref[...].reshape(m, GROUP_W), wo_ref[0], preferred_element_type=F32)
    for i, ref in ((1, f_ref), (2, d_ref)):
        for p in range(N_PAIRS):
            mix = mix + jnp.dot(ref[:, p].reshape(m, LANES), wo_ref[i, p * LANES:(p + 1) * LANES, :],
                                preferred_element_type=F32)
    x = x_ref[...] + g1_ref[...] * mix.reshape(bt, tl, d)
    hb = _rms_mod(x, ng_ref[...], sc_ref[...], sh_ref[...]).reshape(m, d).astype(BF16)
    d_ff = wu_ref.shape[1]
    step = d_ff // 4
    y = jnp.zeros((m, d), F32)
    for j in range(0, d_ff, step):
        u = jnp.maximum(jnp.dot(hb, wu_ref[:, j:j + step], preferred_element_type=F32), 0.0)
        y = y + jnp.dot((u * u).astype(BF16), wd_ref[j:j + step, :], preferred_element_type=F32)
    x = x + g2_ref[...] * y.reshape(bt, tl, d)
    if final_norm:
        x = x * lax.rsqrt(jnp.mean(x * x, axis=-1, keepdims=True) + EPS) * fg_ref[...]
    o_ref[...] = x


def _mlp(x, r, f, dd, g1, sc2, sh2, g2, norm_g, w_out3, w_up, w_down, final_g, bt, tl, final_norm):
    b, l, d = x.shape
    tile = lambda w: pl.BlockSpec((bt, tl, w), lambda i, t: (i, t, 0))
    pairs = pl.BlockSpec((bt, N_PAIRS, tl, LANES), lambda i, t: (i, 0, t, 0))
    mod = pl.BlockSpec((bt, 1, d), lambda i, t: (i, 0, 0))
    return pl.pallas_call(
        functools.partial(_mlp_kernel, final_norm=final_norm),
        grid=(b // bt, l // tl),
        in_specs=[tile(d), tile(GROUP_W), pairs, pairs, mod, mod, mod, mod,
                  _resident((1, d)), _resident(w_out3.shape), _resident(w_up.shape),
                  _resident(w_down.shape), _resident((1, d))],
        out_specs=tile(d),
        out_shape=jax.ShapeDtypeStruct(x.shape, F32),
        compiler_params=_params("arbitrary", "arbitrary"),
        name="out_proj_mlp",
    )(x, r, f, dd, g1, sc2, sh2, g2, norm_g, w_out3, w_up, w_down, final_g)


def _rope_tables(pos, half, batch_reps):
    inv = ROPE_THETA ** (-jnp.arange(half, dtype=F32) / half)
    ang = pos.astype(F32)[:, None] * inv[None, :]
    cos, sin = jnp.cos(ang), jnp.sin(ang)
    reps = GROUP_W // (2 * half)
    cos = jnp.tile(jnp.concatenate([cos, cos], axis=-1), (batch_reps, reps))
    sin = jnp.tile(jnp.concatenate([-sin, sin], axis=-1), (batch_reps, reps))
    return cos, sin


def _pad_axis(a, axis, size):
    pad = [(0, 0)] * a.ndim
    pad[axis] = (0, size - a.shape[axis])
    return jnp.pad(a, pad)


def _forget_cumsum(logf_all):
    lk = logf_all.shape[1]
    x = jnp.transpose(logf_all, (0, 2, 1))
    x = _pad_axis(_pad_axis(x, 1, F_ROWS), 2, -(-lk // LANES) * LANES)
    return _cumsum(x)


def _group_layer(x, mods, tabs, past, lw, lam_init, bt, tl, final_g, final_norm):
    b, l, d = x.shape
    sh1, sc1, g1, sh2, sc2, g2 = mods
    (rq, rk, rv, rg, fq, fk_f, fk_b, fv_f, fv_b, logf_pad,
     dq, dk_f, dk_b, dv_f, dv_b) = _inproj(x, sc1, sh1, lw["norm1_g"], lw["w_main"], lw["w_ff"],
                                           lw["b_ff"], tabs, bt, tl)
    logf = logf_pad[:, :, :N_HEADS]
    if past is None:
        s0 = jnp.zeros((b, N_HEADS, HEAD_DIM, HEAD_DIM), F32)
        f_t = _forget_cumsum(logf)
        fq_cum = jnp.transpose(f_t[:, :N_HEADS + 2, :l], (0, 2, 1))
        fk_rep = jnp.broadcast_to(f_t[:, :N_HEADS].reshape(b, N_HEADS, l // LANES, 1, LANES),
                                  (b, N_HEADS, l // LANES, STRIP, LANES))
        f_out = _fox_prompt(fq, fk_b, fv_b, fq_cum, fk_rep)
        d_out = _diff_prompt(dq, dk_b, dv_b, lw["lam_rows"], lw["diff_norm_g"], lam_init)
    else:
        pfk, pfv, plogf, pdk, pdv, s0 = past
        past_len = pfk.shape[1]
        f_t = _forget_cumsum(jnp.concatenate([plogf, logf], axis=1))
        fq_cum = jnp.transpose(f_t[:, :N_HEADS + 2, past_len:past_len + l], (0, 2, 1))
        new_rows = lambda a: _pad_axis(a, 2, LANES)
        f_out = _fox_sample(fq, pfk.reshape(b, past_len, GROUP_W), pfv.reshape(b, past_len, GROUP_W),
                            new_rows(fk_b), new_rows(fv_b), fq_cum,
                            f_t[:, :, :past_len], f_t[:, :, past_len:past_len + LANES])
        d_out = _diff_sample(dq, pdk.reshape(b, past_len, GROUP_W), pdv.reshape(b, past_len, GROUP_W),
                             new_rows(dk_b), new_rows(dv_b), lw["lam_rows"], lw["diff_norm_g"],
                             lam_init, l)
    r_out, s_pairs = _retention(rq, rk, rv, rg, _state_to_pairs(s0), lw["ret_norm_g"],
                                lw["ret_norm_b"], bt, tl, min(CHUNK, l))
    x = _mlp(x, r_out, f_out, d_out, g1, sc2, sh2, g2, lw["norm2_g"], lw["w_out3"], lw["w_up"],
             lw["w_down"], final_g, bt, tl, final_norm)
    heads = lambda a: a.reshape(b, l, N_HEADS, HEAD_DIM)
    return x, (heads(fk_f), heads(fv_f), logf, heads(dk_f), heads(dv_f), _pairs_to_state(s_pairs))


def kernel(x_prompt, x_sample, cache_fox_k, cache_fox_v, cache_fox_logf, cache_diff_k, cache_diff_v,
           state_ret, c_prompt, c_sample, norm1_g, norm2_g, w_ada, b_ada, w_in, b_forget,
           ret_norm_g, ret_norm_b, lam_q1, lam_k1, lam_q2, lam_k2, diff_norm_g, w_out, w_up, w_down,
           final_g):
    depth, d = norm1_g.shape
    bp, lp, _ = x_prompt.shape
    bs, ls, _ = x_sample.shape
    past_len = cache_fox_k.shape[2]

    rows = -(-(bp + bs) // 8) * 8
    c_all = _pad_axis(jnp.concatenate([c_prompt, c_sample], axis=0), 0, rows)
    mod = _ada(c_all, w_ada, b_ada)

    def mods(li, r0, n):
        return [mod[li, r0:r0 + n, i * d:(i + 1) * d].reshape(n, 1, d) for i in range(6)]

    ff0 = 7 * GROUP_W
    w_main = jnp.concatenate([w_in[:, :, :ff0], w_in[:, :, ff0 + N_HEADS:]], axis=-1).astype(BF16)
    w_ff = _pad_axis(w_in[:, :, ff0:ff0 + N_HEADS], 2, LANES).astype(BF16)
    b_ff = _pad_axis(b_forget, 1, LANES).reshape(depth, 1, LANES)
    w_out3 = w_out.reshape(depth, 3, GROUP_W, d).astype(BF16)
    w_up_b, w_down_b = w_up.astype(BF16), w_down.astype(BF16)
    lam_rows = _pad_axis(_pad_axis(jnp.stack([lam_q1, lam_k1, lam_q2, lam_k2], axis=1), 2, LANES), 1, 8)
    final_row = final_g.reshape(1, d)

    tl_p = min(TOK_TILE, lp)
    p_pos = jnp.arange(lp, dtype=jnp.int32)
    s_pos = past_len + jnp.arange(ls, dtype=jnp.int32)
    tabs_p = _rope_tables(p_pos, HEAD_DIM // 2, 1) + _rope_tables(p_pos, DIFF_QK // 2, 1)
    tabs_s = _rope_tables(s_pos, HEAD_DIM // 2, bs) + _rope_tables(s_pos, DIFF_QK // 2, bs)

    xp, xs = x_prompt, x_sample
    p_new, s_new = [], []
    for li in range(depth):
        lam_init = 0.8 - 0.6 * math.exp(-0.3 * li)
        lw = dict(norm1_g=norm1_g[li].reshape(1, d), norm2_g=norm2_g[li].reshape(1, d),
                  w_main=w_main[li], w_ff=w_ff[li], b_ff=b_ff[li],
                  ret_norm_g=ret_norm_g[li].reshape(1, GROUP_W), ret_norm_b=ret_norm_b[li].reshape(1, GROUP_W),
                  lam_rows=lam_rows[li], diff_norm_g=diff_norm_g[li].reshape(1, GROUP_W),
                  w_out3=w_out3[li], w_up=w_up_b[li], w_down=w_down_b[li])
        last = li == depth - 1
        xp, ps = _group_layer(xp, mods(li, 0, bp), tabs_p, None, lw, lam_init, 1, tl_p, final_row, last)
        past = (cache_fox_k[li], cache_fox_v[li], cache_fox_logf[li], cache_diff_k[li], cache_diff_v[li],
                state_ret[li])
        xs, ss = _group_layer(xs, mods(li, bp, bs), tabs_s, past, lw, lam_init, bs, ls, final_row, last)
        p_new.append(ps)
        s_new.append(ss)

    stack = lambda new, i: jnp.stack([s[i] for s in new])
    return ((xp, xs) + tuple(stack(p_new, i) for i in range(6)) + tuple(stack(s_new, i) for i in range(6)))
```

```python
import functools
import math

import jax
import jax.numpy as jnp
from jax import lax
from jax.experimental import pallas as pl
from jax.experimental.pallas import tpu as pltpu

F32 = jnp.float32
BF16 = jnp.bfloat16

CHUNK = 64
HEAD_DIM = 64
N_HEADS = 6
GROUP_W = N_HEADS * HEAD_DIM
LANES = 128
N_PAIRS = GROUP_W // LANES
DIFF_QK = HEAD_DIM // 2
ROPE_THETA = 10000.0
EPS = 1e-6
MASKED = -1e30
LOG2E = math.log2(math.e)
STRIP = 16
GROUP_ROWS = 128
LOG_G = tuple(math.log1p(-(2.0 ** (-5.0 - h))) for h in range(N_HEADS))
TOK_TILE = 512
ATT_TILE = 512
F_ROWS = 16
VMEM_LIMIT = 56 * 1024 * 1024


def _params(*sem):
    return pltpu.CompilerParams(dimension_semantics=sem, vmem_limit_bytes=VMEM_LIMIT)


def _resident(shape):
    zeros = (0,) * len(shape)
    return pl.BlockSpec(shape, lambda *_: zeros, pipeline_mode=pl.Buffered(1))


def _sigmoid(x):
    return 1.0 / (1.0 + jnp.exp(-x))


def _log2(n):
    assert n > 0 and n & (n - 1) == 0, n
    return n.bit_length() - 1


def _dot_nt(a, b):
    return lax.dot_general(a, b, (((1,), (1,)), ((), ())), preferred_element_type=F32)


def _dot_tn(a, b):
    return lax.dot_general(a, b, (((0,), (0,)), ((), ())), preferred_element_type=F32)


def _rms_mod(x, g, scale, shift):
    y = x * lax.rsqrt(jnp.mean(x * x, axis=-1, keepdims=True) + EPS)
    return (y * g) * (1.0 + scale) + shift


def _lower_half(shape):
    return lax.broadcasted_iota(jnp.int32, shape, len(shape) - 1) < HEAD_DIM


def _half_sums(x, lo):
    a = jnp.sum(jnp.where(lo, x, 0.0), axis=-1, keepdims=True)
    b = jnp.sum(jnp.where(lo, 0.0, x), axis=-1, keepdims=True)
    return jnp.where(lo, a, b)


def _ada_kernel(c_ref, w_ref, b_ref, o_ref):
    c = c_ref[...]
    a = (c * _sigmoid(c)).astype(BF16)
    o_ref[0] = jnp.dot(a, w_ref[0].astype(BF16), preferred_element_type=F32) + b_ref[0]


def _ada(c_all, w_ada, b_ada):
    depth, d, n = w_ada.shape
    rows = c_all.shape[0]
    tn = n // 4
    return pl.pallas_call(
        _ada_kernel,
        grid=(depth, n // tn),
        in_specs=[
            pl.BlockSpec((rows, d), lambda l, j: (0, 0)),
            pl.BlockSpec((1, d, tn), lambda l, j: (l, 0, j)),
            pl.BlockSpec((1, 1, tn), lambda l, j: (l, 0, j)),
        ],
        out_specs=pl.BlockSpec((1, rows, tn), lambda l, j: (l, 0, j)),
        out_shape=jax.ShapeDtypeStruct((depth, rows, n), F32),
        compiler_params=_params("arbitrary", "arbitrary"),
        name="ada_mod",
    )(c_all, w_ada, b_ada.reshape(depth, 1, n))


def _rope(y, cos, sin_signed, half):
    width = y.shape[-1]
    lane = lax.broadcasted_iota(jnp.int32, y.shape, 1)
    ahead = pltpu.roll(y, width - half, axis=1)
    behind = pltpu.roll(y, half, axis=1)
    swapped = jnp.where((lane & half) == 0, ahead, behind)
    return y * cos + swapped * sin_signed


N_INPROJ_IN = 11


def _inproj_kernel(*refs):
    (x_ref, sc_ref, sh_ref, g_ref, wm_ref, wf_ref, bf_ref,
     c64_ref, s64_ref, c32_ref, s32_ref) = refs[:N_INPROJ_IN]
    (rq_o, rk_o, rv_o, rg_o, fq_o, fkf_o, fkb_o, fvf_o, fvb_o, lf_o,
     dq_o, dkf_o, dkb_o, dvf_o, dvb_o) = refs[-15:]
    bt, tl, d = x_ref.shape
    m = bt * tl
    h = _rms_mod(x_ref[...], g_ref[...], sc_ref[...], sh_ref[...])
    hb = h.reshape(m, d).astype(BF16)

    def piece(i):
        return jnp.dot(hb, wm_ref[:, i * GROUP_W:(i + 1) * GROUP_W], preferred_element_type=F32)

    def put(y, *refs):
        for ref in refs:
            if len(ref.shape) == 4 and ref.shape[-1] == LANES:
                for p in range(N_PAIRS):
                    yp = y[:, p * LANES:(p + 1) * LANES]
                    ref[:, p] = yp.reshape(bt, tl, LANES).astype(ref.dtype)
            else:
                ref[...] = y.reshape(ref.shape).astype(ref.dtype)

    c64, s64 = c64_ref[...], s64_ref[...]
    c32, s32 = c32_ref[...], s32_ref[...]
    put(_rope(piece(0), c64, s64, HEAD_DIM // 2), rq_o)
    put(_rope(piece(1), c64, s64, HEAD_DIM // 2), rk_o)
    put(piece(2), rv_o)
    put(piece(3), rg_o)
    put(piece(4) * (HEAD_DIM ** -0.5 * LOG2E), fq_o)
    put(piece(5), fkf_o, fkb_o)
    put(piece(6), fvf_o, fvb_o)
    z = jnp.dot(hb, wf_ref[...], preferred_element_type=F32) + bf_ref[...]
    put(jnp.minimum(z, 0.0) - jnp.log1p(jnp.exp(-jnp.abs(z))), lf_o)
    put(_rope(piece(7), c32, s32, DIFF_QK // 2) * (DIFF_QK ** -0.5 * LOG2E), dq_o)
    put(_rope(piece(8), c32, s32, DIFF_QK // 2), dkf_o, dkb_o)
    put(piece(9), dvf_o, dvb_o)


CACHE_OUTS = (5, 7, 11, 13)


def _inproj(x, scale, shift, gain, w_main, w_ff, b_ff, tabs, bt, tl, layer, depth, cache_rows):
    b, l, d = x.shape
    m = bt * tl
    grid = (b // bt, l // tl)
    tile = lambda w: pl.BlockSpec((bt, tl, w), lambda i, t: (i, t, 0))
    mod = pl.BlockSpec((bt, 1, d), lambda i, t: (i, 0, 0))
    tab = pl.BlockSpec((m, GROUP_W), lambda i, t: (t, 0))
    pairs = (pl.BlockSpec((bt, N_PAIRS, tl, LANES), lambda i, t: (i, 0, t, 0)),
             jax.ShapeDtypeStruct((b, N_PAIRS, l, LANES), BF16))
    flat = lambda w, dt: (tile(w), jax.ShapeDtypeStruct((b, l, w), dt))
    stacked = (pl.BlockSpec((1, bt, tl, GROUP_W), lambda i, t: (layer, i, t, 0)),
               jax.ShapeDtypeStruct((depth, b, l, GROUP_W), F32))
    outs = [
        flat(GROUP_W, BF16), flat(GROUP_W, BF16), flat(GROUP_W, BF16), flat(GROUP_W, F32),
        pairs, stacked, pairs, stacked, pairs,
        flat(LANES, F32),
        pairs, stacked, pairs, stacked, pairs,
    ]
    in_specs = [tile(d), mod, mod, _resident((1, d)), _resident(w_main.shape),
                _resident(w_ff.shape), _resident((1, LANES)), tab, tab, tab, tab]
    assert len(in_specs) == N_INPROJ_IN
    aliases = {}
    if cache_rows is not None:
        in_specs += [pl.BlockSpec(memory_space=pl.ANY)] * len(CACHE_OUTS)
        aliases = {N_INPROJ_IN + n: out for n, out in enumerate(CACHE_OUTS)}
    return pl.pallas_call(
        _inproj_kernel,
        grid=grid,
        in_specs=in_specs,
        out_specs=[spec for spec, _ in outs],
        out_shape=[shape for _, shape in outs],
        input_output_aliases=aliases,
        compiler_params=_params("arbitrary", "arbitrary"),
        name="in_proj",
    )(x, scale, shift, gain, w_main, w_ff, b_ff, *tabs, *(cache_rows or ()))


def _cumsum_kernel(x_ref, o_ref):
    rows, n = x_ref.shape[1], x_ref.shape[2]
    r = lax.broadcasted_iota(jnp.int32, (LANES, LANES), 0)
    c = lax.broadcasted_iota(jnp.int32, (LANES, LANES), 1)
    upper = (r <= c).astype(BF16)
    carry = jnp.zeros((rows, 1), F32)
    for j in range(n // LANES):
        x = x_ref[0, :, j * LANES:(j + 1) * LANES]
        hi = x.astype(BF16)
        r1 = x - hi.astype(F32)
        mid = r1.astype(BF16)
        lo = (r1 - mid.astype(F32)).astype(BF16)
        y = jnp.dot(jnp.concatenate([hi, mid, lo], axis=0), upper, preferred_element_type=F32)
        y = y[:rows] + y[rows:2 * rows] + y[2 * rows:]
        o_ref[0, :, j * LANES:(j + 1) * LANES] = (y + carry) * LOG2E
        carry = carry + y[:, LANES - 1:LANES]


def _cumsum(logf_t):
    b, rows, n = logf_t.shape
    spec = pl.BlockSpec((1, rows, n), lambda i: (i, 0, 0))
    return pl.pallas_call(
        _cumsum_kernel, grid=(b,), in_specs=[spec], out_specs=spec,
        out_shape=jax.ShapeDtypeStruct(logf_t.shape, F32),
        compiler_params=_params("arbitrary"),
        name="forget_cumsum",
    )(logf_t)


def _ret_kernel(q_ref, k_ref, v_ref, g_ref, s0_ref, ng_ref, nb_ref, o_ref, sn_ref, w_scr, s_scr,
                *, chunk):
    bt, tl, _ = q_ref.shape
    m = bt * tl
    t = pl.program_id(1)

    @pl.when((pl.program_id(0) == 0) & (t == 0))
    def _():
        i = lax.broadcasted_iota(jnp.int32, (m, m), 0)
        j = lax.broadcasted_iota(jnp.int32, (m, m), 1)
        visible = ((i >> _log2(tl)) == (j >> _log2(tl))) & ((j >> _log2(chunk)) <= (i >> _log2(chunk)))
        dist = jnp.abs(i - j).astype(F32)
        for h in range(N_HEADS):
            w_scr[h] = jnp.where(visible, jnp.exp(LOG_G[h] * dist), 0.0)

    @pl.when(t == 0)
    def _():
        s_scr[...] = s0_ref[...]

    head = lax.broadcasted_iota(jnp.int32, (1, GROUP_W), 1) >> _log2(HEAD_DIM)
    log_g = jnp.full((1, GROUP_W), LOG_G[N_HEADS - 1], F32)
    for h in range(N_HEADS - 1):
        log_g = jnp.where(head == h, LOG_G[h], log_g)
    pos = (lax.broadcasted_iota(jnp.int32, (m, 1), 0) & (tl - 1)).astype(F32)
    q_w = jnp.exp((pos + 1.0) * log_g)
    k_w = jnp.exp((tl - 1.0 - pos) * log_g) * HEAD_DIM ** -0.5

    qf = q_ref[...].reshape(m, GROUP_W).astype(F32)
    kf = k_ref[...].reshape(m, GROUP_W).astype(F32)
    v = v_ref[...].reshape(m, GROUP_W)
    gate = g_ref[...].reshape(m, GROUP_W)
    qwb = (qf * q_w).astype(BF16)
    kwb = (kf * k_w).astype(BF16)
    ksb = (kf * HEAD_DIM ** -0.5).astype(BF16)
    lo = _lower_half((m, LANES))
    r = lax.broadcasted_iota(jnp.int32, (LANES, LANES), 0) < HEAD_DIM
    c = lax.broadcasted_iota(jnp.int32, (LANES, LANES), 1) < HEAD_DIM
    same_head = r == c

    for p in range(N_PAIRS):
        sl = slice(p * LANES, (p + 1) * LANES)
        qp, vp = qf[:, sl], v[:, sl]
        q2 = jnp.concatenate([jnp.where(lo, qp, 0.0), jnp.where(lo, 0.0, qp)], axis=0).astype(BF16)
        s = _dot_nt(q2, ksb[:, sl])
        probs = jnp.concatenate([(s[:m] * w_scr[2 * p]).astype(BF16),
                                 (s[m:] * w_scr[2 * p + 1]).astype(BF16)], axis=0)
        o2 = jnp.dot(probs, vp, preferred_element_type=F32)
        intra = jnp.where(lo, o2[:m], o2[m:])
        tile_decay = jnp.where(r, math.exp(tl * LOG_G[2 * p]), math.exp(tl * LOG_G[2 * p + 1]))
        cross = []
        for bi in range(bt):
            rows = slice(bi * tl, (bi + 1) * tl)
            state = s_scr[bi, p]
            cross.append(jnp.dot(qwb[rows, sl], state.astype(BF16), preferred_element_type=F32))
            u = _dot_tn(kwb[rows, sl], vp[rows])
            s_scr[bi, p] = tile_decay * state + jnp.where(same_head, u, 0.0)
        y = intra + (jnp.concatenate(cross, axis=0) if bt > 1 else cross[0])
        yc = y - _half_sums(y, lo) * (1.0 / HEAD_DIM)
        var = _half_sums(yc * yc, lo) * (1.0 / HEAD_DIM)
        yn = yc * lax.rsqrt(var + EPS) * ng_ref[:, sl] + nb_ref[:, sl]
        gp = gate[:, sl]
        out = gp * _sigmoid(gp) * yn
        o_ref[:, :, sl] = out.reshape(bt, tl, LANES).astype(o_ref.dtype)

    @pl.when(t == pl.num_programs(1) - 1)
    def _():
        sn_ref[...] = s_scr[...]


def _retention(rq, rk, rv, rg, s0_pairs, norm_g, norm_b, bt, tl, chunk):
    b, l, _ = rq.shape
    m = bt * tl
    tile = pl.BlockSpec((bt, tl, GROUP_W), lambda i, t: (i, t, 0))
    st = pl.BlockSpec((bt, N_PAIRS, LANES, LANES), lambda i, t: (i, 0, 0, 0))
    return pl.pallas_call(
        functools.partial(_ret_kernel, chunk=chunk),
        grid=(b // bt, l // tl),
        in_specs=[tile, tile, tile, tile, st, _resident((1, GROUP_W)), _resident((1, GROUP_W))],
        out_specs=[tile, st],
        out_shape=[jax.ShapeDtypeStruct((b, l, GROUP_W), BF16),
                   jax.ShapeDtypeStruct(s0_pairs.shape, F32)],
        scratch_shapes=[pltpu.VMEM((N_HEADS, m, m), F32),
                        pltpu.VMEM((bt, N_PAIRS, LANES, LANES), F32)],
        compiler_params=_params("arbitrary", "arbitrary"),
        name="retention",
    )(rq, rk, rv, rg, s0_pairs, norm_g, norm_b)


def _state_to_pairs(s):
    b = s.shape[0]
    s = s.reshape(b, N_PAIRS, 2, HEAD_DIM, HEAD_DIM)
    z = jnp.zeros_like(s[:, :, 0])
    top = jnp.concatenate([s[:, :, 0], z], axis=-1)
    bot = jnp.concatenate([z, s[:, :, 1]], axis=-1)
    return jnp.concatenate([top, bot], axis=-2)


def _pairs_to_state(sp):
    b = sp.shape[0]
    a = sp[:, :, :HEAD_DIM, :HEAD_DIM]
    d = sp[:, :, HEAD_DIM:, HEAD_DIM:]
    return jnp.stack([a, d], axis=2).reshape(b, N_HEADS, HEAD_DIM, HEAD_DIM)


def _softmax_step(s, m, l):
    m_new = jnp.maximum(m, jnp.max(s, axis=-1, keepdims=True))
    alpha = jnp.exp2(m - m_new)
    p = jnp.exp2(s - m_new)
    return p, m_new, alpha * l + jnp.sum(p, axis=-1, keepdims=True), alpha


def _stack_masked(qp, n_streams):
    width = LANES // n_streams
    owner = lax.broadcasted_iota(jnp.int32, qp.shape, 1) >> _log2(width)
    return jnp.concatenate([jnp.where(owner == i, qp, 0.0) for i in range(n_streams)],
                           axis=0).astype(BF16)


def _fox_pair(qp, kp, vp, fq_cols, fk_rows, mask, state):
    tq = qp.shape[0]
    s = _dot_nt(_stack_masked(qp, 2), kp)
    probs, new = [], []
    for i in range(2):
        si = s[i * tq:(i + 1) * tq] + (fq_cols[i] - fk_rows[i])
        if mask is not None:
            si = jnp.where(mask, si, MASKED)
        p, m_new, l_new, alpha = _softmax_step(si, state[i][0], state[i][1])
        probs.append(p.astype(BF16))
        new.append((m_new, l_new, alpha))
    pv = jnp.dot(jnp.concatenate(probs, axis=0), vp, preferred_element_type=F32)
    return [(m_new, l_new, alpha * state[i][2] + pv[i * tq:(i + 1) * tq])
            for i, (m_new, l_new, alpha) in enumerate(new)]


def _diff_pair(qp, kp, vp, mask, state):
    tq = qp.shape[0]
    s = _dot_nt(_stack_masked(qp, 4), kp)
    probs, new = [], []
    for i in range(4):
        si = s[i * tq:(i + 1) * tq]
        if mask is not None:
            si = jnp.where(mask, si, MASKED)
        p, m_new, l_new, alpha = _softmax_step(si, state[i][0], state[i][1])
        probs.append(p.astype(BF16))
        new.append((m_new, l_new, alpha))
    pv = jnp.dot(jnp.concatenate(probs, axis=0), vp, preferred_element_type=F32)
    return [(m_new, l_new, alpha * state[i][2] + pv[i * tq:(i + 1) * tq])
            for i, (m_new, l_new, alpha) in enumerate(new)]


def _fresh(n, tq):
    return [(jnp.full((tq, 1), MASKED, F32), jnp.zeros((tq, 1), F32), jnp.zeros((tq, LANES), F32))
            for _ in range(n)]


def _fox_out(state, lo):
    (_, la, acca), (_, lb, accb) = state
    return jnp.where(lo, acca / la, accb / lb)


def _lam(lam_ref, lam_init):
    a = lam_ref[...]
    t1 = jnp.sum(a[0:1] * a[1:2], axis=-1, keepdims=True)
    t2 = jnp.sum(a[2:3] * a[3:4], axis=-1, keepdims=True)
    return jnp.exp(t1) - jnp.exp(t2) + lam_init


def _diff_norm(o, lo, gain, lam_init):
    ms = _half_sums(o * o, lo) * (1.0 / HEAD_DIM)
    return (o * lax.rsqrt(ms + EPS) * gain) * (1.0 - lam_init)


def _diff_out(state, lo, lam, gain, lam_init):
    (_, l0, a0), (_, l1, a1), (_, l2, a2), (_, l3, a3) = state
    o = jnp.where(lo, a0 / l0, a2 / l2) - lam * jnp.where(lo, a1 / l1, a3 / l3)
    return _diff_norm(o, lo, gain, lam_init)


def _rows(r0, n):
    return slice(r0, r0 + n)


def _tile_update(scr, q_scr, k, v, tq, key_bias_fn, row_bias_scr, visibility):
    s_scr, p_scr, m_scr, c_scr, a_scr, acc_scr = scr
    n_rows, tk = s_scr.shape
    n_chunks = tk // LANES
    chunk = lambda j: slice(j * LANES, (j + 1) * LANES)
    for r0 in range(0, n_rows, GROUP_ROWS):
        rows = _rows(r0, GROUP_ROWS)
        s_scr[rows, :] = _dot_nt(q_scr[rows, :], k)
    for r0 in range(0, n_rows, STRIP):
        rows = _rows(r0, STRIP)
        stream, row_in_stream = divmod(r0, tq)
        live = []
        for j in range(n_chunks):
            vis = visibility(row_in_stream, j)
            if vis is None:
                continue
            t = s_scr[rows, chunk(j)]
            if key_bias_fn is not None:
                t = t - key_bias_fn(stream, j)
            if vis is not True:
                t = jnp.where(vis, t, MASKED)
            if key_bias_fn is not None or vis is not True:
                s_scr[rows, chunk(j)] = t
            live.append(t)
        top = functools.reduce(jnp.maximum, live)
        top = jnp.broadcast_to(jnp.max(top, axis=-1, keepdims=True), (STRIP, LANES))
        m_old = m_scr[rows, :]
        if row_bias_scr is not None:
            row_bias = row_bias_scr[rows, :]
            m_new = jnp.maximum(m_old, top + row_bias)
            c_scr[rows, :] = m_new - row_bias
        else:
            m_new = jnp.maximum(m_old, top)
            c_scr[rows, :] = m_new
        a_scr[rows, :] = jnp.exp2(m_old - m_new)
        m_scr[rows, :] = m_new
    for r0 in range(0, n_rows, STRIP):
        rows = _rows(r0, STRIP)
        centre = c_scr[rows, :]
        for j in range(n_chunks):
            if visibility(r0 % tq, j) is None:
                p_scr[rows, chunk(j)] = jnp.zeros((STRIP, LANES), BF16)
            else:
                p_scr[rows, chunk(j)] = jnp.exp2((s_scr[rows, chunk(j)] - centre).astype(BF16))
    ones_col = (lax.broadcasted_iota(jnp.int32, v.shape, 1) == 0).astype(BF16)
    v_ext = jnp.concatenate([v, ones_col], axis=1)
    for r0 in range(0, n_rows, GROUP_ROWS):
        rows = _rows(r0, GROUP_ROWS)
        alpha = a_scr[rows, :]
        acc_scr[rows, :] = (jnp.concatenate([alpha, alpha], axis=1) * acc_scr[rows, :]
                            + jnp.dot(p_scr[rows, :], v_ext, preferred_element_type=F32))


def _all_visible(row0, j):
    return True


def _causal_visibility(row0, j):
    if j * LANES > row0 + STRIP - 1:
        return None
    if j * LANES + LANES - 1 <= row0:
        return True
    rel = (lax.broadcasted_iota(jnp.int32, (STRIP, LANES), 1)
           - lax.broadcasted_iota(jnp.int32, (STRIP, LANES), 0))
    return rel <= row0 - j * LANES


def _chunk_visibility(row0, j):
    q_chunk = row0 // CHUNK
    first, last = j * (LANES // CHUNK), (j + 1) * (LANES // CHUNK) - 1
    if first > q_chunk:
        return None
    if last <= q_chunk:
        return True
    return (lax.broadcasted_iota(jnp.int32, (STRIP, LANES), 1) >> _log2(CHUNK)) <= q_chunk - first


def _init_att_state(m_scr, acc_scr):
    m_scr[...] = jnp.full(m_scr.shape, MASKED, F32)
    acc_scr[...] = jnp.zeros(acc_scr.shape, F32)


def _normalised(acc_scr):
    acc = acc_scr[...]
    return acc[:, :LANES] / acc[:, LANES:LANES + 1]


def _fox_kernel(q_ref, k_ref, v_ref, fq_ref, fk_ref, o_ref,
                q_scr, fq_scr, m_scr, c_scr, a_scr, acc_scr, s_scr, p_scr):
    tq = q_ref.shape[2]
    tk = s_scr.shape[1]
    qi = pl.program_id(1)
    scr = (s_scr, p_scr, m_scr, c_scr, a_scr, acc_scr)
    chunks = tk // LANES
    head_col = lax.broadcasted_iota(jnp.int32, fq_ref.shape[1:], 1)
    lo = _lower_half((tq, LANES))

    def pair(p, carry):
        q_scr[...] = _stack_masked(q_ref[0, p].astype(F32), 2)
        for half in range(2):
            col = jnp.sum(jnp.where(head_col == 2 * p + half, fq_ref[0], 0.0), axis=-1, keepdims=True)
            fq_scr[half * tq:(half + 1) * tq, :] = jnp.broadcast_to(col, (tq, LANES))
        _init_att_state(m_scr, acc_scr)

        def tile(ki, visibility):
            keys = pl.ds(pl.multiple_of(ki * tk, tk), tk)
            key_bias = lambda stream, j: fk_ref[0, 2 * p + stream, ki * chunks + j]
            _tile_update(scr, q_scr, k_ref[0, p, keys, :], v_ref[0, p, keys, :], tq,
                         key_bias, fq_scr, visibility)

        def full_tile(ki, c):
            tile(ki, _all_visible)
            return c

        lax.fori_loop(0, qi, full_tile, 0)
        tile(qi, _causal_visibility)
        o = _normalised(acc_scr)
        o_ref[0, p] = jnp.where(lo, o[:tq], o[tq:]).astype(o_ref.dtype)
        return carry

    lax.fori_loop(0, N_PAIRS, pair, 0)


def _diff_kernel(q_ref, k_ref, v_ref, lam_ref, g_ref, o_ref,
                 q_scr, m_scr, c_scr, a_scr, acc_scr, s_scr, p_scr, *, lam_init):
    tq = q_ref.shape[2]
    tk = s_scr.shape[1]
    qi = pl.program_id(1)
    scr = (s_scr, p_scr, m_scr, c_scr, a_scr, acc_scr)
    lo = _lower_half((tq, LANES))
    lam = _lam(lam_ref, lam_init)

    def pair(p, carry):
        q_scr[...] = _stack_masked(q_ref[0, p].astype(F32), 4)
        _init_att_state(m_scr, acc_scr)

        def tile(ki, visibility):
            keys = pl.ds(pl.multiple_of(ki * tk, tk), tk)
            _tile_update(scr, q_scr, k_ref[0, p, keys, :], v_ref[0, p, keys, :], tq,
                         None, None, visibility)

        def full_tile(ki, c):
            tile(ki, _all_visible)
            return c

        lax.fori_loop(0, qi, full_tile, 0)
        tile(qi, _chunk_visibility)
        o = _normalised(acc_scr)
        o = (jnp.where(lo, o[:tq], o[2 * tq:3 * tq]) - lam * jnp.where(lo, o[tq:2 * tq], o[3 * tq:]))
        o_ref[0, p] = _diff_norm(o, lo, g_ref[p], lam_init).astype(o_ref.dtype)
        return carry

    lax.fori_loop(0, N_PAIRS, pair, 0)


def _prompt_att_call(kernel_fn, name, n_streams, q, k, v, extra, extra_specs, row_bias):
    b, _, l, _ = q.shape
    tq = tk = min(ATT_TILE, l)
    assert l % tq == 0 and tq % CHUNK == 0
    rows = n_streams * tq
    qspec = pl.BlockSpec((1, N_PAIRS, tq, LANES), lambda i, t: (i, 0, t, 0))
    kspec = pl.BlockSpec((1, N_PAIRS, l, LANES), lambda i, t: (i, 0, 0, 0), pipeline_mode=pl.Buffered(1))
    stats = pltpu.VMEM((rows, LANES), F32)
    return pl.pallas_call(
        kernel_fn, grid=(b, l // tq),
        in_specs=[qspec, kspec, kspec] + extra_specs,
        out_specs=qspec,
        out_shape=jax.ShapeDtypeStruct(q.shape, BF16),
        scratch_shapes=[pltpu.VMEM((rows, LANES), BF16)] + ([stats] if row_bias else [])
        + [stats, stats, stats, pltpu.VMEM((rows, 2 * LANES), F32),
           pltpu.VMEM((rows, tk), F32), pltpu.VMEM((rows, tk), BF16)],
        compiler_params=_params("arbitrary", "arbitrary"),
        name=name,
    )(q, k, v, *extra)


def _fox_prompt(q, k, v, fq, fk_rep):
    tq = min(ATT_TILE, q.shape[2])
    specs = [pl.BlockSpec((1, tq, fq.shape[-1]), lambda i, t: (i, t, 0)),
             pl.BlockSpec((1,) + fk_rep.shape[1:], lambda i, t: (i, 0, 0, 0, 0),
                          pipeline_mode=pl.Buffered(1))]
    return _prompt_att_call(_fox_kernel, "fox_attention", 2, q, k, v, (fq, fk_rep), specs, True)


def _diff_prompt(q, k, v, lam_rows, gain, lam_init):
    gain_pairs = gain.reshape(N_PAIRS, 1, LANES)
    specs = [_resident(lam_rows.shape), _resident(gain_pairs.shape)]
    return _prompt_att_call(functools.partial(_diff_kernel, lam_init=lam_init), "diff_attention", 4,
                            q, k, v, (lam_rows, gain_pairs), specs, False)


def _fox_sample_kernel(q_ref, kp_ref, vp_ref, kn_ref, vn_ref, fq_ref, fkp_ref, fkn_ref, o_ref):
    tq, tn = q_ref.shape[2], kn_ref.shape[2]
    kp, vp = kp_ref[0].astype(BF16), vp_ref[0].astype(BF16)
    fq, fkp, fkn = fq_ref[0], fkp_ref[0], fkn_ref[0]
    causal = lax.broadcasted_iota(jnp.int32, (tq, tn), 1) <= lax.broadcasted_iota(jnp.int32, (tq, tn), 0)
    lo = _lower_half((tq, LANES))
    for p in range(N_PAIRS):
        sl = slice(p * LANES, (p + 1) * LANES)
        idx = (2 * p, 2 * p + 1)
        fq_cols = [fq[:, h:h + 1] for h in idx]
        qf = q_ref[0, p].astype(F32)
        state = _fox_pair(qf, kp[:, sl], vp[:, sl], fq_cols,
                          [fkp[h:h + 1, :] for h in idx], None, _fresh(2, tq))
        state = _fox_pair(qf, kn_ref[0, p], vn_ref[0, p], fq_cols,
                          [fkn[h:h + 1, :] for h in idx], causal, state)
        o_ref[0, p] = _fox_out(state, lo).astype(o_ref.dtype)


def _diff_sample_kernel(q_ref, kp_ref, vp_ref, kn_ref, vn_ref, lam_ref, g_ref, o_ref,
                        *, lam_init, past_len, n_new):
    tq, tn = q_ref.shape[2], kn_ref.shape[2]
    kp, vp = kp_ref[0].astype(BF16), vp_ref[0].astype(BF16)
    col = lax.broadcasted_iota(jnp.int32, (tq, tn), 1)
    row = lax.broadcasted_iota(jnp.int32, (tq, tn), 0)
    shift = _log2(CHUNK)
    visible = (((past_len + col) >> shift) <= ((past_len + row) >> shift)) & (col < n_new)
    lo = _lower_half((tq, LANES))
    lam = _lam(lam_ref, lam_init)
    for p in range(N_PAIRS):
        sl = slice(p * LANES, (p + 1) * LANES)
        qf = q_ref[0, p].astype(F32)
        state = _diff_pair(qf, kp[:, sl], vp[:, sl], None, _fresh(4, tq))
        state = _diff_pair(qf, kn_ref[0, p], vn_ref[0, p], visible, state)
        o_ref[0, p] = _diff_out(state, lo, lam, g_ref[:, sl], lam_init).astype(o_ref.dtype)


def _sample_specs(q, k_past, k_new):
    b = q.shape[0]
    pairs = lambda a: pl.BlockSpec((1,) + a.shape[1:], lambda i: (i, 0, 0, 0))
    past = pl.BlockSpec((1, k_past.shape[1], GROUP_W), lambda i: (i, 0, 0))
    return b, pairs(q), past, pairs(k_new)


def _fox_sample(q, k_past, v_past, k_new, v_new, fq, fk_past, fk_new):
    b, qspec, pspec, nspec = _sample_specs(q, k_past, k_new)
    frow = lambda a: pl.BlockSpec((1,) + a.shape[1:], lambda i: (i, 0, 0))
    return pl.pallas_call(
        _fox_sample_kernel, grid=(b,),
        in_specs=[qspec, pspec, pspec, nspec, nspec, frow(fq), frow(fk_past), frow(fk_new)],
        out_specs=qspec,
        out_shape=jax.ShapeDtypeStruct(q.shape, BF16),
        compiler_params=_params("arbitrary"),
        name="fox_attention_cached",
    )(q, k_past, v_past, k_new, v_new, fq, fk_past, fk_new)


def _diff_sample(q, k_past, v_past, k_new, v_new, lam_rows, gain, lam_init, n_new):
    b, qspec, pspec, nspec = _sample_specs(q, k_past, k_new)
    past_len = k_past.shape[1]
    assert past_len % CHUNK == 0
    return pl.pallas_call(
        functools.partial(_diff_sample_kernel, lam_init=lam_init, past_len=past_len, n_new=n_new),
        grid=(b,),
        in_specs=[qspec, pspec, pspec, nspec, nspec, _resident(lam_rows.shape), _resident((1, GROUP_W))],
        out_specs=qspec,
        out_shape=jax.ShapeDtypeStruct(q.shape, BF16),
        compiler_params=_params("arbitrary"),
        name="diff_attention_cached",
    )(q, k_past, v_past, k_new, v_new, lam_rows, gain)


def _mlp_kernel(x_ref, r_ref, f_ref, d_ref, g1_ref, sc_ref, sh_ref, g2_ref, ng_ref,
                wo_ref, wu_ref, wd_ref, fg_ref, o_ref, *, final_norm):
    bt, tl, d = x_ref.shape
    m = bt * tl
    mix = jnp.dot(r_ref[...].reshape(m, GROUP_W), wo_ref[0], preferred_element_type=F32)
    for i, ref in ((1, f_ref), (2, d_ref)):
        for p in range(N_PAIRS):
            mix = mix + jnp.dot(ref[:, p].reshape(m, LANES), wo_ref[i, p * LANES:(p + 1) * LANES, :],
                                preferred_element_type=F32)
    x = x_ref[...] + g1_ref[...] * mix.reshape(bt, tl, d)
    hb = _rms_mod(x, ng_ref[...], sc_ref[...], sh_ref[...]).reshape(m, d).astype(BF16)
    d_ff = wu_ref.shape[1]
    step = d_ff // 4
    y = jnp.zeros((m, d), F32)
    for j in range(0, d_ff, step):
        u = jnp.maximum(jnp.dot(hb, wu_ref[:, j:j + step], preferred_element_type=F32), 0.0)
        y = y + jnp.dot((u * u).astype(BF16), wd_ref[j:j + step, :], preferred_element_type=F32)
    x = x + g2_ref[...] * y.reshape(bt, tl, d)
    if final_norm:
        x = x * lax.rsqrt(jnp.mean(x * x, axis=-1, keepdims=True) + EPS) * fg_ref[...]
    o_ref[...] = x


def _mlp(x, r, f, dd, g1, sc2, sh2, g2, norm_g, w_out3, w_up, w_down, final_g, bt, tl, final_norm):
    b, l, d = x.shape
    tile = lambda w: pl.BlockSpec((bt, tl, w), lambda i, t: (i, t, 0))
    pairs = pl.BlockSpec((bt, N_PAIRS, tl, LANES), lambda i, t: (i, 0, t, 0))
    mod = pl.BlockSpec((bt, 1, d), lambda i, t: (i, 0, 0))
    return pl.pallas_call(
        functools.partial(_mlp_kernel, final_norm=final_norm),
        grid=(b // bt, l // tl),
        in_specs=[tile(d), tile(GROUP_W), pairs, pairs, mod, mod, mod, mod,
                  _resident((1, d)), _resident(w_out3.shape), _resident(w_up.shape),
                  _resident(w_down.shape), _resident((1, d))],
        out_specs=tile(d),
        out_shape=jax.ShapeDtypeStruct(x.shape, F32),
        compiler_params=_params("arbitrary", "arbitrary"),
        name="out_proj_mlp",
    )(x, r, f, dd, g1, sc2, sh2, g2, norm_g, w_out3, w_up, w_down, final_g)


def _rope_tables(pos, half, batch_reps):
    inv = ROPE_THETA ** (-jnp.arange(half, dtype=F32) / half)
    ang = pos.astype(F32)[:, None] * inv[None, :]
    cos, sin = jnp.cos(ang), jnp.sin(ang)
    reps = GROUP_W // (2 * half)
    cos = jnp.tile(jnp.concatenate([cos, cos], axis=-1), (batch_reps, reps))
    sin = jnp.tile(jnp.concatenate([-sin, sin], axis=-1), (batch_reps, reps))
    return cos, sin


def _pad_axis(a, axis, size):
    pad = [(0, 0)] * a.ndim
    pad[axis] = (0, size - a.shape[axis])
    return jnp.pad(a, pad)


def _forget_cumsum(logf_all):
    lk = logf_all.shape[1]
    x = jnp.transpose(logf_all, (0, 2, 1))
    x = _pad_axis(_pad_axis(x, 1, F_ROWS), 2, -(-lk // LANES) * LANES)
    return _cumsum(x)


def _group_layer(x, mods, tabs, past, lw, lam_init, bt, tl, final_g, final_norm, layer, depth,
                 cache_rows):
    b, l, d = x.shape
    sh1, sc1, g1, sh2, sc2, g2 = mods
    (rq, rk, rv, rg, fq, fk_f, fk_b, fv_f, fv_b, logf_pad,
     dq, dk_f, dk_b, dv_f, dv_b) = _inproj(x, sc1, sh1, lw["norm1_g"], lw["w_main"], lw["w_ff"],
                                           lw["b_ff"], tabs, bt, tl, layer, depth, cache_rows)
    logf = logf_pad[:, :, :N_HEADS]
    if past is None:
        s0 = jnp.zeros((b, N_HEADS, HEAD_DIM, HEAD_DIM), F32)
        f_t = _forget_cumsum(logf)
        fq_cum = jnp.transpose(f_t[:, :N_HEADS + 2, :l], (0, 2, 1))
        fk_rep = jnp.broadcast_to(f_t[:, :N_HEADS].reshape(b, N_HEADS, l // LANES, 1, LANES),
                                  (b, N_HEADS, l // LANES, STRIP, LANES))
        f_out = _fox_prompt(fq, fk_b, fv_b, fq_cum, fk_rep)
        d_out = _diff_prompt(dq, dk_b, dv_b, lw["lam_rows"], lw["diff_norm_g"], lam_init)
    else:
        pfk, pfv, plogf, pdk, pdv, s0 = past
        past_len = pfk.shape[1]
        f_t = _forget_cumsum(jnp.concatenate([plogf, logf], axis=1))
        fq_cum = jnp.transpose(f_t[:, :N_HEADS + 2, past_len:past_len + l], (0, 2, 1))
        new_rows = lambda a: _pad_axis(a, 2, LANES)
        f_out = _fox_sample(fq, pfk.reshape(b, past_len, GROUP_W), pfv.reshape(b, past_len, GROUP_W),
                            new_rows(fk_b), new_rows(fv_b), fq_cum,
                            f_t[:, :, :past_len], f_t[:, :, past_len:past_len + LANES])
        d_out = _diff_sample(dq, pdk.reshape(b, past_len, GROUP_W), pdv.reshape(b, past_len, GROUP_W),
                             new_rows(dk_b), new_rows(dv_b), lw["lam_rows"], lw["diff_norm_g"],
                             lam_init, l)
    r_out, s_pairs = _retention(rq, rk, rv, rg, _state_to_pairs(s0), lw["ret_norm_g"],
                                lw["ret_norm_b"], bt, tl, min(CHUNK, l))
    x = _mlp(x, r_out, f_out, d_out, g1, sc2, sh2, g2, lw["norm2_g"], lw["w_out3"], lw["w_up"],
             lw["w_down"], final_g, bt, tl, final_norm)
    return x, (fk_f, fv_f, dk_f, dv_f), logf, _pairs_to_state(s_pairs)


def kernel(x_prompt, x_sample, cache_fox_k, cache_fox_v, cache_fox_logf, cache_diff_k, cache_diff_v,
           state_ret, c_prompt, c_sample, norm1_g, norm2_g, w_ada, b_ada, w_in, b_forget,
           ret_norm_g, ret_norm_b, lam_q1, lam_k1, lam_q2, lam_k2, diff_norm_g, w_out, w_up, w_down,
           final_g):
    depth, d = norm1_g.shape
    bp, lp, _ = x_prompt.shape
    bs, ls, _ = x_sample.shape
    past_len = cache_fox_k.shape[2]

    rows = -(-(bp + bs) // 8) * 8
    c_all = _pad_axis(jnp.concatenate([c_prompt, c_sample], axis=0), 0, rows)
    mod = _ada(c_all, w_ada, b_ada)

    def mods(li, r0, n):
        return [mod[li, r0:r0 + n, i * d:(i + 1) * d].reshape(n, 1, d) for i in range(6)]

    ff0 = 7 * GROUP_W
    w_main = jnp.concatenate([w_in[:, :, :ff0], w_in[:, :, ff0 + N_HEADS:]], axis=-1).astype(BF16)
    w_ff = _pad_axis(w_in[:, :, ff0:ff0 + N_HEADS], 2, LANES).astype(BF16)
    b_ff = _pad_axis(b_forget, 1, LANES).reshape(depth, 1, LANES)
    w_out3 = w_out.reshape(depth, 3, GROUP_W, d).astype(BF16)
    w_up_b, w_down_b = w_up.astype(BF16), w_down.astype(BF16)
    lam_rows = _pad_axis(_pad_axis(jnp.stack([lam_q1, lam_k1, lam_q2, lam_k2], axis=1), 2, LANES), 1, 8)
    final_row = final_g.reshape(1, d)

    tl_p = min(TOK_TILE, lp)
    p_pos = jnp.arange(lp, dtype=jnp.int32)
    s_pos = past_len + jnp.arange(ls, dtype=jnp.int32)
    tabs_p = _rope_tables(p_pos, HEAD_DIM // 2, 1) + _rope_tables(p_pos, DIFF_QK // 2, 1)
    tabs_s = _rope_tables(s_pos, HEAD_DIM // 2, bs) + _rope_tables(s_pos, DIFF_QK // 2, bs)

    xp, xs = x_prompt, x_sample
    p_rows, s_rows = None, None
    p_logf, p_state, s_logf, s_state = [], [], [], []
    for li in range(depth):
        lam_init = 0.8 - 0.6 * math.exp(-0.3 * li)
        lw = dict(norm1_g=norm1_g[li].reshape(1, d), norm2_g=norm2_g[li].reshape(1, d),
                  w_main=w_main[li], w_ff=w_ff[li], b_ff=b_ff[li],
                  ret_norm_g=ret_norm_g[li].reshape(1, GROUP_W), ret_norm_b=ret_norm_b[li].reshape(1, GROUP_W),
                  lam_rows=lam_rows[li], diff_norm_g=diff_norm_g[li].reshape(1, GROUP_W),
                  w_out3=w_out3[li], w_up=w_up_b[li], w_down=w_down_b[li])
        last = li == depth - 1
        xp, p_rows, logf, state = _group_layer(xp, mods(li, 0, bp), tabs_p, None, lw, lam_init, 1, tl_p,
                                               final_row, last, li, depth, p_rows)
        p_logf.append(logf)
        p_state.append(state)
        past = (cache_fox_k[li], cache_fox_v[li], cache_fox_logf[li], cache_diff_k[li], cache_diff_v[li],
                state_ret[li])
        xs, s_rows, logf, state = _group_layer(xs, mods(li, bp, bs), tabs_s, past, lw, lam_init, bs, ls,
                                               final_row, last, li, depth, s_rows)
        s_logf.append(logf)
        s_state.append(state)

    def group_outputs(rows, logf, state):
        fk, fv, dk, dv = (a.reshape(a.shape[:3] + (N_HEADS, HEAD_DIM)) for a in rows)
        return fk, fv, jnp.stack(logf), dk, dv, jnp.stack(state)

    return (xp, xs) + group_outputs(p_rows, p_logf, p_state) + group_outputs(s_rows, s_logf, s_state)
```

```python
import functools
import math

import jax
import jax.numpy as jnp
from jax import lax
from jax.experimental import pallas as pl
from jax.experimental.pallas import tpu as pltpu

F32 = jnp.float32
BF16 = jnp.bfloat16

CHUNK = 64
HEAD_DIM = 64
N_HEADS = 6
GROUP_W = N_HEADS * HEAD_DIM
LANES = 128
N_PAIRS = GROUP_W // LANES
DIFF_QK = HEAD_DIM // 2
ROPE_THETA = 10000.0
EPS = 1e-6
MASKED = -1e30
LOG2E = math.log2(math.e)
STRIP = 16
GROUP_ROWS = 128
LOG_G = tuple(math.log1p(-(2.0 ** (-5.0 - h))) for h in range(N_HEADS))
TOK_TILE = 512
ATT_TILE = 512
F_ROWS = 16
VMEM_LIMIT = 56 * 1024 * 1024


def _params(*sem):
    return pltpu.CompilerParams(dimension_semantics=sem, vmem_limit_bytes=VMEM_LIMIT)


def _resident(shape):
    zeros = (0,) * len(shape)
    return pl.BlockSpec(shape, lambda *_: zeros, pipeline_mode=pl.Buffered(1))


def _sigmoid(x):
    return 1.0 / (1.0 + jnp.exp(-x))


def _log2(n):
    assert n > 0 and n & (n - 1) == 0, n
    return n.bit_length() - 1


def _dot_nt(a, b):
    return lax.dot_general(a, b, (((1,), (1,)), ((), ())), preferred_element_type=F32)


def _dot_tn(a, b):
    return lax.dot_general(a, b, (((0,), (0,)), ((), ())), preferred_element_type=F32)


def _rms_mod(x, g, scale, shift):
    y = x * lax.rsqrt(jnp.mean(x * x, axis=-1, keepdims=True) + EPS)
    return (y * g) * (1.0 + scale) + shift


def _lower_half(shape):
    return lax.broadcasted_iota(jnp.int32, shape, len(shape) - 1) < HEAD_DIM


def _half_sums(x, lo):
    a = jnp.sum(jnp.where(lo, x, 0.0), axis=-1, keepdims=True)
    b = jnp.sum(jnp.where(lo, 0.0, x), axis=-1, keepdims=True)
    return jnp.where(lo, a, b)


def _ada_kernel(c_ref, w_ref, b_ref, o_ref):
    c = c_ref[...]
    a = (c * _sigmoid(c)).astype(BF16)
    o_ref[0] = jnp.dot(a, w_ref[0].astype(BF16), preferred_element_type=F32) + b_ref[0]


def _ada(c_all, w_ada, b_ada):
    depth, d, n = w_ada.shape
    rows = c_all.shape[0]
    tn = n // 4
    return pl.pallas_call(
        _ada_kernel,
        grid=(depth, n // tn),
        in_specs=[
            pl.BlockSpec((rows, d), lambda l, j: (0, 0)),
            pl.BlockSpec((1, d, tn), lambda l, j: (l, 0, j)),
            pl.BlockSpec((1, 1, tn), lambda l, j: (l, 0, j)),
        ],
        out_specs=pl.BlockSpec((1, rows, tn), lambda l, j: (l, 0, j)),
        out_shape=jax.ShapeDtypeStruct((depth, rows, n), F32),
        compiler_params=_params("arbitrary", "arbitrary"),
        name="ada_mod",
    )(c_all, w_ada, b_ada.reshape(depth, 1, n))


def _rope(y, cos, sin_signed, half):
    width = y.shape[-1]
    lane = lax.broadcasted_iota(jnp.int32, y.shape, 1)
    ahead = pltpu.roll(y, width - half, axis=1)
    behind = pltpu.roll(y, half, axis=1)
    swapped = jnp.where((lane & half) == 0, ahead, behind)
    return y * cos + swapped * sin_signed


N_INPROJ_IN = 11


def _inproj_kernel(*refs):
    (x_ref, sc_ref, sh_ref, g_ref, wm_ref, wf_ref, bf_ref,
     c64_ref, s64_ref, c32_ref, s32_ref) = refs[:N_INPROJ_IN]
    (rq_o, rk_o, rv_o, rg_o, fq_o, fkf_o, fkb_o, fvf_o, fvb_o, lf_o,
     dq_o, dkf_o, dkb_o, dvf_o, dvb_o) = refs[-15:]
    bt, tl, d = x_ref.shape
    m = bt * tl
    h = _rms_mod(x_ref[...], g_ref[...], sc_ref[...], sh_ref[...])
    hb = h.reshape(m, d).astype(BF16)

    def piece(i):
        return jnp.dot(hb, wm_ref[:, i * GROUP_W:(i + 1) * GROUP_W], preferred_element_type=F32)

    def put(y, *refs):
        for ref in refs:
            if len(ref.shape) == 4 and ref.shape[-1] == LANES:
                for p in range(N_PAIRS):
                    yp = y[:, p * LANES:(p + 1) * LANES]
                    ref[:, p] = yp.reshape(bt, tl, LANES).astype(ref.dtype)
            else:
                ref[...] = y.reshape(ref.shape).astype(ref.dtype)

    c64, s64 = c64_ref[...], s64_ref[...]
    c32, s32 = c32_ref[...], s32_ref[...]
    put(_rope(piece(0), c64, s64, HEAD_DIM // 2), rq_o)
    put(_rope(piece(1), c64, s64, HEAD_DIM // 2), rk_o)
    put(piece(2), rv_o)
    put(piece(3), rg_o)
    put(piece(4) * (HEAD_DIM ** -0.5 * LOG2E), fq_o)
    put(piece(5), fkf_o, fkb_o)
    put(piece(6), fvf_o, fvb_o)
    z = jnp.dot(hb, wf_ref[...], preferred_element_type=F32) + bf_ref[...]
    put(jnp.minimum(z, 0.0) - jnp.log1p(jnp.exp(-jnp.abs(z))), lf_o)
    put(_rope(piece(7), c32, s32, DIFF_QK // 2) * (DIFF_QK ** -0.5 * LOG2E), dq_o)
    put(_rope(piece(8), c32, s32, DIFF_QK // 2), dkf_o, dkb_o)
    put(piece(9), dvf_o, dvb_o)


CACHE_OUTS = (5, 7, 11, 13)


def _inproj(x, scale, shift, gain, w_main, w_ff, b_ff, tabs, bt, tl, layer, depth, cache_rows):
    b, l, d = x.shape
    m = bt * tl
    grid = (b // bt, l // tl)
    tile = lambda w: pl.BlockSpec((bt, tl, w), lambda i, t: (i, t, 0))
    mod = pl.BlockSpec((bt, 1, d), lambda i, t: (i, 0, 0))
    tab = pl.BlockSpec((m, GROUP_W), lambda i, t: (t, 0))
    pairs = (pl.BlockSpec((bt, N_PAIRS, tl, LANES), lambda i, t: (i, 0, t, 0)),
             jax.ShapeDtypeStruct((b, N_PAIRS, l, LANES), BF16))
    flat = lambda w, dt: (tile(w), jax.ShapeDtypeStruct((b, l, w), dt))
    stacked = (pl.BlockSpec((1, bt, tl, GROUP_W), lambda i, t: (layer, i, t, 0)),
               jax.ShapeDtypeStruct((depth, b, l, GROUP_W), F32))
    outs = [
        flat(GROUP_W, BF16), flat(GROUP_W, BF16), flat(GROUP_W, BF16), flat(GROUP_W, F32),
        pairs, stacked, pairs, stacked, pairs,
        flat(LANES, F32),
        pairs, stacked, pairs, stacked, pairs,
    ]
    in_specs = [tile(d), mod, mod, _resident((1, d)), _resident(w_main.shape),
                _resident(w_ff.shape), _resident((1, LANES)), tab, tab, tab, tab]
    assert len(in_specs) == N_INPROJ_IN
    aliases = {}
    if cache_rows is not None:
        in_specs += [pl.BlockSpec(memory_space=pl.ANY)] * len(CACHE_OUTS)
        aliases = {N_INPROJ_IN + n: out for n, out in enumerate(CACHE_OUTS)}
    return pl.pallas_call(
        _inproj_kernel,
        grid=grid,
        in_specs=in_specs,
        out_specs=[spec for spec, _ in outs],
        out_shape=[shape for _, shape in outs],
        input_output_aliases=aliases,
        compiler_params=_params("arbitrary", "arbitrary"),
        name="in_proj",
    )(x, scale, shift, gain, w_main, w_ff, b_ff, *tabs, *(cache_rows or ()))


def _cumsum_kernel(x_ref, o_ref):
    rows, n = x_ref.shape[1], x_ref.shape[2]
    r = lax.broadcasted_iota(jnp.int32, (LANES, LANES), 0)
    c = lax.broadcasted_iota(jnp.int32, (LANES, LANES), 1)
    upper = (r <= c).astype(BF16)
    carry = jnp.zeros((rows, 1), F32)
    for j in range(n // LANES):
        x = x_ref[0, :, j * LANES:(j + 1) * LANES]
        hi = x.astype(BF16)
        r1 = x - hi.astype(F32)
        mid = r1.astype(BF16)
        lo = (r1 - mid.astype(F32)).astype(BF16)
        y = jnp.dot(jnp.concatenate([hi, mid, lo], axis=0), upper, preferred_element_type=F32)
        y = y[:rows] + y[rows:2 * rows] + y[2 * rows:]
        o_ref[0, :, j * LANES:(j + 1) * LANES] = (y + carry) * LOG2E
        carry = carry + y[:, LANES - 1:LANES]


def _cumsum(logf_t):
    b, rows, n = logf_t.shape
    spec = pl.BlockSpec((1, rows, n), lambda i: (i, 0, 0))
    return pl.pallas_call(
        _cumsum_kernel, grid=(b,), in_specs=[spec], out_specs=spec,
        out_shape=jax.ShapeDtypeStruct(logf_t.shape, F32),
        compiler_params=_params("arbitrary"),
        name="forget_cumsum",
    )(logf_t)


def _ret_kernel(q_ref, k_ref, v_ref, g_ref, s0_ref, ng_ref, nb_ref, o_ref, sn_ref, w_scr, s_scr,
                *, chunk):
    bt, tl, _ = q_ref.shape
    m = bt * tl
    t = pl.program_id(1)

    @pl.when((pl.program_id(0) == 0) & (t == 0))
    def _():
        i = lax.broadcasted_iota(jnp.int32, (m, m), 0)
        j = lax.broadcasted_iota(jnp.int32, (m, m), 1)
        visible = ((i >> _log2(tl)) == (j >> _log2(tl))) & ((j >> _log2(chunk)) <= (i >> _log2(chunk)))
        dist = jnp.abs(i - j).astype(F32)
        for h in range(N_HEADS):
            w_scr[h] = jnp.where(visible, jnp.exp(LOG_G[h] * dist), 0.0)

    @pl.when(t == 0)
    def _():
        s_scr[...] = s0_ref[...]

    head = lax.broadcasted_iota(jnp.int32, (1, GROUP_W), 1) >> _log2(HEAD_DIM)
    log_g = jnp.full((1, GROUP_W), LOG_G[N_HEADS - 1], F32)
    for h in range(N_HEADS - 1):
        log_g = jnp.where(head == h, LOG_G[h], log_g)
    pos = (lax.broadcasted_iota(jnp.int32, (m, 1), 0) & (tl - 1)).astype(F32)
    q_w = jnp.exp((pos + 1.0) * log_g)
    k_w = jnp.exp((tl - 1.0 - pos) * log_g) * HEAD_DIM ** -0.5

    qf = q_ref[...].reshape(m, GROUP_W).astype(F32)
    kf = k_ref[...].reshape(m, GROUP_W).astype(F32)
    v = v_ref[...].reshape(m, GROUP_W)
    gate = g_ref[...].reshape(m, GROUP_W)
    qwb = (qf * q_w).astype(BF16)
    kwb = (kf * k_w).astype(BF16)
    ksb = (kf * HEAD_DIM ** -0.5).astype(BF16)
    lo = _lower_half((m, LANES))
    r = lax.broadcasted_iota(jnp.int32, (LANES, LANES), 0) < HEAD_DIM
    c = lax.broadcasted_iota(jnp.int32, (LANES, LANES), 1) < HEAD_DIM
    same_head = r == c

    for p in range(N_PAIRS):
        sl = slice(p * LANES, (p + 1) * LANES)
        qp, vp = qf[:, sl], v[:, sl]
        q2 = jnp.concatenate([jnp.where(lo, qp, 0.0), jnp.where(lo, 0.0, qp)], axis=0).astype(BF16)
        s = _dot_nt(q2, ksb[:, sl])
        probs = jnp.concatenate([(s[:m] * w_scr[2 * p]).astype(BF16),
                                 (s[m:] * w_scr[2 * p + 1]).astype(BF16)], axis=0)
        o2 = jnp.dot(probs, vp, preferred_element_type=F32)
        intra = jnp.where(lo, o2[:m], o2[m:])
        tile_decay = jnp.where(r, math.exp(tl * LOG_G[2 * p]), math.exp(tl * LOG_G[2 * p + 1]))
        cross = []
        for bi in range(bt):
            rows = slice(bi * tl, (bi + 1) * tl)
            state = s_scr[bi, p]
            cross.append(jnp.dot(qwb[rows, sl], state.astype(BF16), preferred_element_type=F32))
            u = _dot_tn(kwb[rows, sl], vp[rows])
            s_scr[bi, p] = tile_decay * state + jnp.where(same_head, u, 0.0)
        y = intra + (jnp.concatenate(cross, axis=0) if bt > 1 else cross[0])
        yc = y - _half_sums(y, lo) * (1.0 / HEAD_DIM)
        var = _half_sums(yc * yc, lo) * (1.0 / HEAD_DIM)
        yn = yc * lax.rsqrt(var + EPS) * ng_ref[:, sl] + nb_ref[:, sl]
        gp = gate[:, sl]
        out = gp * _sigmoid(gp) * yn
        o_ref[:, :, sl] = out.reshape(bt, tl, LANES).astype(o_ref.dtype)

    @pl.when(t == pl.num_programs(1) - 1)
    def _():
        sn_ref[...] = s_scr[...]


def _retention(rq, rk, rv, rg, s0_pairs, norm_g, norm_b, bt, tl, chunk):
    b, l, _ = rq.shape
    m = bt * tl
    tile = pl.BlockSpec((bt, tl, GROUP_W), lambda i, t: (i, t, 0))
    st = pl.BlockSpec((bt, N_PAIRS, LANES, LANES), lambda i, t: (i, 0, 0, 0))
    return pl.pallas_call(
        functools.partial(_ret_kernel, chunk=chunk),
        grid=(b // bt, l // tl),
        in_specs=[tile, tile, tile, tile, st, _resident((1, GROUP_W)), _resident((1, GROUP_W))],
        out_specs=[tile, st],
        out_shape=[jax.ShapeDtypeStruct((b, l, GROUP_W), BF16),
                   jax.ShapeDtypeStruct(s0_pairs.shape, F32)],
        scratch_shapes=[pltpu.VMEM((N_HEADS, m, m), F32),
                        pltpu.VMEM((bt, N_PAIRS, LANES, LANES), F32)],
        compiler_params=_params("arbitrary", "arbitrary"),
        name="retention",
    )(rq, rk, rv, rg, s0_pairs, norm_g, norm_b)


def _state_to_pairs(s):
    b = s.shape[0]
    s = s.reshape(b, N_PAIRS, 2, HEAD_DIM, HEAD_DIM)
    z = jnp.zeros_like(s[:, :, 0])
    top = jnp.concatenate([s[:, :, 0], z], axis=-1)
    bot = jnp.concatenate([z, s[:, :, 1]], axis=-1)
    return jnp.concatenate([top, bot], axis=-2)


def _pairs_to_state(sp):
    b = sp.shape[0]
    a = sp[:, :, :HEAD_DIM, :HEAD_DIM]
    d = sp[:, :, HEAD_DIM:, HEAD_DIM:]
    return jnp.stack([a, d], axis=2).reshape(b, N_HEADS, HEAD_DIM, HEAD_DIM)


def _softmax_step(s, m, l):
    m_new = jnp.maximum(m, jnp.max(s, axis=-1, keepdims=True))
    alpha = jnp.exp2(m - m_new)
    p = jnp.exp2(s - m_new)
    return p, m_new, alpha * l + jnp.sum(p, axis=-1, keepdims=True), alpha


def _stack_masked(qp, n_streams):
    width = LANES // n_streams
    owner = lax.broadcasted_iota(jnp.int32, qp.shape, 1) >> _log2(width)
    return jnp.concatenate([jnp.where(owner == i, qp, 0.0) for i in range(n_streams)],
                           axis=0).astype(BF16)


def _pair_block(qp, kp, vp, biases, mask, state, frames_last=False):
    n, tq = len(state), qp.shape[0]
    q_stack = _stack_masked(qp, n)
    s = jnp.dot(q_stack, kp, preferred_element_type=F32) if frames_last else _dot_nt(q_stack, kp)
    probs, new = [], []
    for i in range(n):
        si = s[i * tq:(i + 1) * tq]
        if biases is not None:
            si = si + biases[i]
        if mask is not None:
            si = jnp.where(mask, si, MASKED)
        p, m_new, l_new, alpha = _softmax_step(si, state[i][0], state[i][1])
        probs.append(p.astype(BF16))
        new.append((m_new, l_new, alpha))
    probs = jnp.concatenate(probs, axis=0)
    pv = _dot_nt(probs, vp) if frames_last else jnp.dot(probs, vp, preferred_element_type=F32)
    return [(m_new, l_new, alpha * state[i][2] + pv[i * tq:(i + 1) * tq])
            for i, (m_new, l_new, alpha) in enumerate(new)]


def _cached_pair(ref, p):
    pair = ref[0, 0, 2 * p:2 * p + 2]
    return pair.reshape(LANES, pair.shape[-1]).astype(BF16)


def _fresh(n, tq):
    return [(jnp.full((tq, 1), MASKED, F32), jnp.zeros((tq, 1), F32), jnp.zeros((tq, LANES), F32))
            for _ in range(n)]


def _fox_out(state, lo):
    (_, la, acca), (_, lb, accb) = state
    return jnp.where(lo, acca / la, accb / lb)


def _lam(lam_ref, lam_init):
    a = lam_ref[...]
    t1 = jnp.sum(a[0:1] * a[1:2], axis=-1, keepdims=True)
    t2 = jnp.sum(a[2:3] * a[3:4], axis=-1, keepdims=True)
    return jnp.exp(t1) - jnp.exp(t2) + lam_init


def _diff_norm(o, lo, gain, lam_init):
    ms = _half_sums(o * o, lo) * (1.0 / HEAD_DIM)
    return (o * lax.rsqrt(ms + EPS) * gain) * (1.0 - lam_init)


def _diff_out(state, lo, lam, gain, lam_init):
    (_, l0, a0), (_, l1, a1), (_, l2, a2), (_, l3, a3) = state
    o = jnp.where(lo, a0 / l0, a2 / l2) - lam * jnp.where(lo, a1 / l1, a3 / l3)
    return _diff_norm(o, lo, gain, lam_init)


def _rows(r0, n):
    return slice(r0, r0 + n)


def _tile_update(scr, q_scr, k, v, tq, key_bias_fn, row_bias_scr, visibility):
    s_scr, p_scr, m_scr, c_scr, a_scr, acc_scr = scr
    n_rows, tk = s_scr.shape
    n_chunks = tk // LANES
    chunk = lambda j: slice(j * LANES, (j + 1) * LANES)
    for r0 in range(0, n_rows, GROUP_ROWS):
        rows = _rows(r0, GROUP_ROWS)
        s_scr[rows, :] = _dot_nt(q_scr[rows, :], k)
    for r0 in range(0, n_rows, STRIP):
        rows = _rows(r0, STRIP)
        stream, row_in_stream = divmod(r0, tq)
        live = []
        for j in range(n_chunks):
            vis = visibility(row_in_stream, j)
            if vis is None:
                continue
            t = s_scr[rows, chunk(j)]
            if key_bias_fn is not None:
                t = t - key_bias_fn(stream, j)
            if vis is not True:
                t = jnp.where(vis, t, MASKED)
            if key_bias_fn is not None or vis is not True:
                s_scr[rows, chunk(j)] = t
            live.append(t)
        top = functools.reduce(jnp.maximum, live)
        top = jnp.broadcast_to(jnp.max(top, axis=-1, keepdims=True), (STRIP, LANES))
        m_old = m_scr[rows, :]
        if row_bias_scr is not None:
            row_bias = row_bias_scr[rows, :]
            m_new = jnp.maximum(m_old, top + row_bias)
            c_scr[rows, :] = m_new - row_bias
        else:
            m_new = jnp.maximum(m_old, top)
            c_scr[rows, :] = m_new
        a_scr[rows, :] = jnp.exp2(m_old - m_new)
        m_scr[rows, :] = m_new
    for r0 in range(0, n_rows, STRIP):
        rows = _rows(r0, STRIP)
        centre = c_scr[rows, :]
        for j in range(n_chunks):
            if visibility(r0 % tq, j) is None:
                p_scr[rows, chunk(j)] = jnp.zeros((STRIP, LANES), BF16)
            else:
                p_scr[rows, chunk(j)] = jnp.exp2((s_scr[rows, chunk(j)] - centre).astype(BF16))
    ones_col = (lax.broadcasted_iota(jnp.int32, v.shape, 1) == 0).astype(BF16)
    v_ext = jnp.concatenate([v, ones_col], axis=1)
    for r0 in range(0, n_rows, GROUP_ROWS):
        rows = _rows(r0, GROUP_ROWS)
        alpha = a_scr[rows, :]
        acc_scr[rows, :] = (jnp.concatenate([alpha, alpha], axis=1) * acc_scr[rows, :]
                            + jnp.dot(p_scr[rows, :], v_ext, preferred_element_type=F32))


def _all_visible(row0, j):
    return True


def _causal_visibility(row0, j):
    if j * LANES > row0 + STRIP - 1:
        return None
    if j * LANES + LANES - 1 <= row0:
        return True
    rel = (lax.broadcasted_iota(jnp.int32, (STRIP, LANES), 1)
           - lax.broadcasted_iota(jnp.int32, (STRIP, LANES), 0))
    return rel <= row0 - j * LANES


def _chunk_visibility(row0, j):
    q_chunk = row0 // CHUNK
    first, last = j * (LANES // CHUNK), (j + 1) * (LANES // CHUNK) - 1
    if first > q_chunk:
        return None
    if last <= q_chunk:
        return True
    return (lax.broadcasted_iota(jnp.int32, (STRIP, LANES), 1) >> _log2(CHUNK)) <= q_chunk - first


def _init_att_state(m_scr, acc_scr):
    m_scr[...] = jnp.full(m_scr.shape, MASKED, F32)
    acc_scr[...] = jnp.zeros(acc_scr.shape, F32)


def _normalised(acc_scr):
    acc = acc_scr[...]
    return acc[:, :LANES] / acc[:, LANES:LANES + 1]


def _fox_kernel(q_ref, k_ref, v_ref, fq_ref, fk_ref, o_ref,
                q_scr, fq_scr, m_scr, c_scr, a_scr, acc_scr, s_scr, p_scr):
    tq = q_ref.shape[2]
    tk = s_scr.shape[1]
    qi = pl.program_id(1)
    scr = (s_scr, p_scr, m_scr, c_scr, a_scr, acc_scr)
    chunks = tk // LANES
    head_col = lax.broadcasted_iota(jnp.int32, fq_ref.shape[1:], 1)
    lo = _lower_half((tq, LANES))

    def pair(p, carry):
        q_scr[...] = _stack_masked(q_ref[0, p].astype(F32), 2)
        for half in range(2):
            col = jnp.sum(jnp.where(head_col == 2 * p + half, fq_ref[0], 0.0), axis=-1, keepdims=True)
            fq_scr[half * tq:(half + 1) * tq, :] = jnp.broadcast_to(col, (tq, LANES))
        _init_att_state(m_scr, acc_scr)

        def tile(ki, visibility):
            keys = pl.ds(pl.multiple_of(ki * tk, tk), tk)
            key_bias = lambda stream, j: fk_ref[0, 2 * p + stream, ki * chunks + j]
            _tile_update(scr, q_scr, k_ref[0, p, keys, :], v_ref[0, p, keys, :], tq,
                         key_bias, fq_scr, visibility)

        def full_tile(ki, c):
            tile(ki, _all_visible)
            return c

        lax.fori_loop(0, qi, full_tile, 0)
        tile(qi, _causal_visibility)
        o = _normalised(acc_scr)
        o_ref[0, p] = jnp.where(lo, o[:tq], o[tq:]).astype(o_ref.dtype)
        return carry

    lax.fori_loop(0, N_PAIRS, pair, 0)


def _diff_kernel(q_ref, k_ref, v_ref, lam_ref, g_ref, o_ref,
                 q_scr, m_scr, c_scr, a_scr, acc_scr, s_scr, p_scr, *, lam_init):
    tq = q_ref.shape[2]
    tk = s_scr.shape[1]
    qi = pl.program_id(1)
    scr = (s_scr, p_scr, m_scr, c_scr, a_scr, acc_scr)
    lo = _lower_half((tq, LANES))
    lam = _lam(lam_ref, lam_init)

    def pair(p, carry):
        q_scr[...] = _stack_masked(q_ref[0, p].astype(F32), 4)
        _init_att_state(m_scr, acc_scr)

        def tile(ki, visibility):
            keys = pl.ds(pl.multiple_of(ki * tk, tk), tk)
            _tile_update(scr, q_scr, k_ref[0, p, keys, :], v_ref[0, p, keys, :], tq,
                         None, None, visibility)

        def full_tile(ki, c):
            tile(ki, _all_visible)
            return c

        lax.fori_loop(0, qi, full_tile, 0)
        tile(qi, _chunk_visibility)
        o = _normalised(acc_scr)
        o = (jnp.where(lo, o[:tq], o[2 * tq:3 * tq]) - lam * jnp.where(lo, o[tq:2 * tq], o[3 * tq:]))
        o_ref[0, p] = _diff_norm(o, lo, g_ref[p], lam_init).astype(o_ref.dtype)
        return carry

    lax.fori_loop(0, N_PAIRS, pair, 0)


def _prompt_att_call(kernel_fn, name, n_streams, q, k, v, extra, extra_specs, row_bias):
    b, _, l, _ = q.shape
    tq = tk = min(ATT_TILE, l)
    assert l % tq == 0 and tq % CHUNK == 0
    rows = n_streams * tq
    qspec = pl.BlockSpec((1, N_PAIRS, tq, LANES), lambda i, t: (i, 0, t, 0))
    kspec = pl.BlockSpec((1, N_PAIRS, l, LANES), lambda i, t: (i, 0, 0, 0), pipeline_mode=pl.Buffered(1))
    stats = pltpu.VMEM((rows, LANES), F32)
    return pl.pallas_call(
        kernel_fn, grid=(b, l // tq),
        in_specs=[qspec, kspec, kspec] + extra_specs,
        out_specs=qspec,
        out_shape=jax.ShapeDtypeStruct(q.shape, BF16),
        scratch_shapes=[pltpu.VMEM((rows, LANES), BF16)] + ([stats] if row_bias else [])
        + [stats, stats, stats, pltpu.VMEM((rows, 2 * LANES), F32),
           pltpu.VMEM((rows, tk), F32), pltpu.VMEM((rows, tk), BF16)],
        compiler_params=_params("arbitrary", "arbitrary"),
        name=name,
    )(q, k, v, *extra)


def _fox_prompt(q, k, v, fq, fk_rep):
    tq = min(ATT_TILE, q.shape[2])
    specs = [pl.BlockSpec((1, tq, fq.shape[-1]), lambda i, t: (i, t, 0)),
             pl.BlockSpec((1,) + fk_rep.shape[1:], lambda i, t: (i, 0, 0, 0, 0),
                          pipeline_mode=pl.Buffered(1))]
    return _prompt_att_call(_fox_kernel, "fox_attention", 2, q, k, v, (fq, fk_rep), specs, True)


def _diff_prompt(q, k, v, lam_rows, gain, lam_init):
    gain_pairs = gain.reshape(N_PAIRS, 1, LANES)
    specs = [_resident(lam_rows.shape), _resident(gain_pairs.shape)]
    return _prompt_att_call(functools.partial(_diff_kernel, lam_init=lam_init), "diff_attention", 4,
                            q, k, v, (lam_rows, gain_pairs), specs, False)


def _fox_sample_kernel(q_ref, kp_ref, vp_ref, kn_ref, vn_ref, fq_ref, fkp_ref, fkn_ref, o_ref):
    tq, tn = q_ref.shape[2], kn_ref.shape[2]
    fq, fkp, fkn = fq_ref[0], fkp_ref[0], fkn_ref[0]
    causal = lax.broadcasted_iota(jnp.int32, (tq, tn), 1) <= lax.broadcasted_iota(jnp.int32, (tq, tn), 0)
    lo = _lower_half((tq, LANES))
    for p in range(N_PAIRS):
        idx = (2 * p, 2 * p + 1)
        qf = q_ref[0, p].astype(F32)
        state = _pair_block(qf, _cached_pair(kp_ref, p), _cached_pair(vp_ref, p),
                            [fq[:, h:h + 1] - fkp[h:h + 1, :] for h in idx], None, _fresh(2, tq),
                            frames_last=True)
        state = _pair_block(qf, kn_ref[0, p], vn_ref[0, p],
                            [fq[:, h:h + 1] - fkn[h:h + 1, :] for h in idx], causal, state)
        o_ref[0, p] = _fox_out(state, lo).astype(o_ref.dtype)


def _diff_sample_kernel(q_ref, kp_ref, vp_ref, kn_ref, vn_ref, lam_ref, g_ref, o_ref,
                        *, lam_init, past_len, n_new):
    tq, tn = q_ref.shape[2], kn_ref.shape[2]
    col = lax.broadcasted_iota(jnp.int32, (tq, tn), 1)
    row = lax.broadcasted_iota(jnp.int32, (tq, tn), 0)
    shift = _log2(CHUNK)
    visible = (((past_len + col) >> shift) <= ((past_len + row) >> shift)) & (col < n_new)
    lo = _lower_half((tq, LANES))
    lam = _lam(lam_ref, lam_init)
    for p in range(N_PAIRS):
        sl = slice(p * LANES, (p + 1) * LANES)
        qf = q_ref[0, p].astype(F32)
        state = _pair_block(qf, _cached_pair(kp_ref, p), _cached_pair(vp_ref, p), None, None,
                            _fresh(4, tq), frames_last=True)
        state = _pair_block(qf, kn_ref[0, p], vn_ref[0, p], None, visible, state)
        o_ref[0, p] = _diff_out(state, lo, lam, g_ref[:, sl], lam_init).astype(o_ref.dtype)


def _sample_specs(q, k_past, k_new, layer):
    b = q.shape[0]
    pairs = lambda a: pl.BlockSpec((1,) + a.shape[1:], lambda i: (i, 0, 0, 0))
    past = pl.BlockSpec((1, 1) + k_past.shape[2:], lambda i: (layer, i, 0, 0, 0))
    return b, pairs(q), past, pairs(k_new)


def _fox_sample(q, k_past, v_past, k_new, v_new, fq, fk_past, fk_new, layer):
    b, qspec, pspec, nspec = _sample_specs(q, k_past, k_new, layer)
    frow = lambda a: pl.BlockSpec((1,) + a.shape[1:], lambda i: (i, 0, 0))
    return pl.pallas_call(
        _fox_sample_kernel, grid=(b,),
        in_specs=[qspec, pspec, pspec, nspec, nspec, frow(fq), frow(fk_past), frow(fk_new)],
        out_specs=qspec,
        out_shape=jax.ShapeDtypeStruct(q.shape, BF16),
        compiler_params=_params("arbitrary"),
        name="fox_attention_cached",
    )(q, k_past, v_past, k_new, v_new, fq, fk_past, fk_new)


def _diff_sample(q, k_past, v_past, k_new, v_new, lam_rows, gain, lam_init, n_new, layer):
    b, qspec, pspec, nspec = _sample_specs(q, k_past, k_new, layer)
    past_len = k_past.shape[-1]
    assert past_len % CHUNK == 0
    return pl.pallas_call(
        functools.partial(_diff_sample_kernel, lam_init=lam_init, past_len=past_len, n_new=n_new),
        grid=(b,),
        in_specs=[qspec, pspec, pspec, nspec, nspec, _resident(lam_rows.shape), _resident((1, GROUP_W))],
        out_specs=qspec,
        out_shape=jax.ShapeDtypeStruct(q.shape, BF16),
        compiler_params=_params("arbitrary"),
        name="diff_attention_cached",
    )(q, k_past, v_past, k_new, v_new, lam_rows, gain)


def _mlp_kernel(x_ref, r_ref, f_ref, d_ref, g1_ref, sc_ref, sh_ref, g2_ref, ng_ref,
                wo_ref, wu_ref, wd_ref, fg_ref, o_ref, *, final_norm):
    bt, tl, d = x_ref.shape
    m = bt * tl
    mix = jnp.dot(r_ref[...].reshape(m, GROUP_W), wo_ref[0], preferred_element_type=F32)
    for i, ref in ((1, f_ref), (2, d_ref)):
        for p in range(N_PAIRS):
            mix = mix + jnp.dot(ref[:, p].reshape(m, LANES), wo_ref[i, p * LANES:(p + 1) * LANES, :],
                                preferred_element_type=F32)
    x = x_ref[...] + g1_ref[...] * mix.reshape(bt, tl, d)
    hb = _rms_mod(x, ng_ref[...], sc_ref[...], sh_ref[...]).reshape(m, d).astype(BF16)
    d_ff = wu_ref.shape[1]
    step = d_ff // 4
    y = jnp.zeros((m, d), F32)
    for j in range(0, d_ff, step):
        u = jnp.maximum(jnp.dot(hb, wu_ref[:, j:j + step], preferred_element_type=F32), 0.0)
        y = y + jnp.dot((u * u).astype(BF16), wd_ref[j:j + step, :], preferred_element_type=F32)
    x = x + g2_ref[...] * y.reshape(bt, tl, d)
    if final_norm:
        x = x * lax.rsqrt(jnp.mean(x * x, axis=-1, keepdims=True) + EPS) * fg_ref[...]
    o_ref[...] = x


def _mlp(x, r, f, dd, g1, sc2, sh2, g2, norm_g, w_out3, w_up, w_down, final_g, bt, tl, final_norm):
    b, l, d = x.shape
    tile = lambda w: pl.BlockSpec((bt, tl, w), lambda i, t: (i, t, 0))
    pairs = pl.BlockSpec((bt, N_PAIRS, tl, LANES), lambda i, t: (i, 0, t, 0))
    mod = pl.BlockSpec((bt, 1, d), lambda i, t: (i, 0, 0))
    return pl.pallas_call(
        functools.partial(_mlp_kernel, final_norm=final_norm),
        grid=(b // bt, l // tl),
        in_specs=[tile(d), tile(GROUP_W), pairs, pairs, mod, mod, mod, mod,
                  _resident((1, d)), _resident(w_out3.shape), _resident(w_up.shape),
                  _resident(w_down.shape), _resident((1, d))],
        out_specs=tile(d),
        out_shape=jax.ShapeDtypeStruct(x.shape, F32),
        compiler_params=_params("arbitrary", "arbitrary"),
        name="out_proj_mlp",
    )(x, r, f, dd, g1, sc2, sh2, g2, norm_g, w_out3, w_up, w_down, final_g)


def _rope_tables(pos, half, batch_reps):
    inv = ROPE_THETA ** (-jnp.arange(half, dtype=F32) / half)
    ang = pos.astype(F32)[:, None] * inv[None, :]
    cos, sin = jnp.cos(ang), jnp.sin(ang)
    reps = GROUP_W // (2 * half)
    cos = jnp.tile(jnp.concatenate([cos, cos], axis=-1), (batch_reps, reps))
    sin = jnp.tile(jnp.concatenate([-sin, sin], axis=-1), (batch_reps, reps))
    return cos, sin


def _pad_axis(a, axis, size):
    pad = [(0, 0)] * a.ndim
    pad[axis] = (0, size - a.shape[axis])
    return jnp.pad(a, pad)


def _forget_cumsum(logf_all):
    lk = logf_all.shape[1]
    x = jnp.transpose(logf_all, (0, 2, 1))
    x = _pad_axis(_pad_axis(x, 1, F_ROWS), 2, -(-lk // LANES) * LANES)
    return _cumsum(x)


def _group_layer(x, mods, tabs, past, lw, lam_init, bt, tl, final_g, final_norm, layer, depth,
                 cache_rows):
    b, l, d = x.shape
    sh1, sc1, g1, sh2, sc2, g2 = mods
    (rq, rk, rv, rg, fq, fk_f, fk_b, fv_f, fv_b, logf_pad,
     dq, dk_f, dk_b, dv_f, dv_b) = _inproj(x, sc1, sh1, lw["norm1_g"], lw["w_main"], lw["w_ff"],
                                           lw["b_ff"], tabs, bt, tl, layer, depth, cache_rows)
    logf = logf_pad[:, :, :N_HEADS]
    if past is None:
        s0 = jnp.zeros((b, N_HEADS, HEAD_DIM, HEAD_DIM), F32)
        f_t = _forget_cumsum(logf)
        fq_cum = jnp.transpose(f_t[:, :N_HEADS + 2, :l], (0, 2, 1))
        fk_rep = jnp.broadcast_to(f_t[:, :N_HEADS].reshape(b, N_HEADS, l // LANES, 1, LANES),
                                  (b, N_HEADS, l // LANES, STRIP, LANES))
        f_out = _fox_prompt(fq, fk_b, fv_b, fq_cum, fk_rep)
        d_out = _diff_prompt(dq, dk_b, dv_b, lw["lam_rows"], lw["diff_norm_g"], lam_init)
    else:
        pfk, pfv, plogf, pdk, pdv, s0 = past
        past_len = pfk.shape[-1]
        f_t = _forget_cumsum(jnp.concatenate([plogf, logf], axis=1))
        fq_cum = jnp.transpose(f_t[:, :N_HEADS + 2, past_len:past_len + l], (0, 2, 1))
        new_rows = lambda a: _pad_axis(a, 2, LANES)
        f_out = _fox_sample(fq, pfk, pfv, new_rows(fk_b), new_rows(fv_b), fq_cum,
                            f_t[:, :, :past_len], f_t[:, :, past_len:past_len + LANES], layer)
        d_out = _diff_sample(dq, pdk, pdv, new_rows(dk_b), new_rows(dv_b), lw["lam_rows"],
                             lw["diff_norm_g"], lam_init, l, layer)
    r_out, s_pairs = _retention(rq, rk, rv, rg, _state_to_pairs(s0), lw["ret_norm_g"],
                                lw["ret_norm_b"], bt, tl, min(CHUNK, l))
    x = _mlp(x, r_out, f_out, d_out, g1, sc2, sh2, g2, lw["norm2_g"], lw["w_out3"], lw["w_up"],
             lw["w_down"], final_g, bt, tl, final_norm)
    return x, (fk_f, fv_f, dk_f, dv_f), logf, _pairs_to_state(s_pairs)


def kernel(x_prompt, x_sample, cache_fox_k, cache_fox_v, cache_fox_logf, cache_diff_k, cache_diff_v,
           state_ret, c_prompt, c_sample, norm1_g, norm2_g, w_ada, b_ada, w_in, b_forget,
           ret_norm_g, ret_norm_b, lam_q1, lam_k1, lam_q2, lam_k2, diff_norm_g, w_out, w_up, w_down,
           final_g):
    depth, d = norm1_g.shape
    bp, lp, _ = x_prompt.shape
    bs, ls, _ = x_sample.shape
    past_len = cache_fox_k.shape[2]

    rows = -(-(bp + bs) // 8) * 8
    c_all = _pad_axis(jnp.concatenate([c_prompt, c_sample], axis=0), 0, rows)
    mod = _ada(c_all, w_ada, b_ada)

    def mods(li, r0, n):
        return [mod[li, r0:r0 + n, i * d:(i + 1) * d].reshape(n, 1, d) for i in range(6)]

    ff0 = 7 * GROUP_W
    w_main = jnp.concatenate([w_in[:, :, :ff0], w_in[:, :, ff0 + N_HEADS:]], axis=-1).astype(BF16)
    w_ff = _pad_axis(w_in[:, :, ff0:ff0 + N_HEADS], 2, LANES).astype(BF16)
    b_ff = _pad_axis(b_forget, 1, LANES).reshape(depth, 1, LANES)
    w_out3 = w_out.reshape(depth, 3, GROUP_W, d).astype(BF16)
    w_up_b, w_down_b = w_up.astype(BF16), w_down.astype(BF16)
    lam_rows = _pad_axis(_pad_axis(jnp.stack([lam_q1, lam_k1, lam_q2, lam_k2], axis=1), 2, LANES), 1, 8)
    final_row = final_g.reshape(1, d)

    tl_p = min(TOK_TILE, lp)
    p_pos = jnp.arange(lp, dtype=jnp.int32)
    s_pos = past_len + jnp.arange(ls, dtype=jnp.int32)
    tabs_p = _rope_tables(p_pos, HEAD_DIM // 2, 1) + _rope_tables(p_pos, DIFF_QK // 2, 1)
    tabs_s = _rope_tables(s_pos, HEAD_DIM // 2, bs) + _rope_tables(s_pos, DIFF_QK // 2, bs)

    frames_last = lambda a: jnp.transpose(a, (0, 1, 3, 4, 2))
    fox_k_t, fox_v_t = frames_last(cache_fox_k), frames_last(cache_fox_v)
    diff_k_t, diff_v_t = frames_last(cache_diff_k), frames_last(cache_diff_v)

    xp, xs = x_prompt, x_sample
    p_rows, s_rows = None, None
    p_logf, p_state, s_logf, s_state = [], [], [], []
    for li in range(depth):
        lam_init = 0.8 - 0.6 * math.exp(-0.3 * li)
        lw = dict(norm1_g=norm1_g[li].reshape(1, d), norm2_g=norm2_g[li].reshape(1, d),
                  w_main=w_main[li], w_ff=w_ff[li], b_ff=b_ff[li],
                  ret_norm_g=ret_norm_g[li].reshape(1, GROUP_W), ret_norm_b=ret_norm_b[li].reshape(1, GROUP_W),
                  lam_rows=lam_rows[li], diff_norm_g=diff_norm_g[li].reshape(1, GROUP_W),
                  w_out3=w_out3[li], w_up=w_up_b[li], w_down=w_down_b[li])
        last = li == depth - 1
        xp, p_rows, logf, state = _group_layer(xp, mods(li, 0, bp), tabs_p, None, lw, lam_init, 1, tl_p,
                                               final_row, last, li, depth, p_rows)
        p_logf.append(logf)
        p_state.append(state)
        past = (fox_k_t, fox_v_t, cache_fox_logf[li], diff_k_t, diff_v_t, state_ret[li])
        xs, s_rows, logf, state = _group_layer(xs, mods(li, bp, bs), tabs_s, past, lw, lam_init, bs, ls,
                                               final_row, last, li, depth, s_rows)
        s_logf.append(logf)
        s_state.append(state)

    def group_outputs(rows, logf, state):
        fk, fv, dk, dv = (a.reshape(a.shape[:3] + (N_HEADS, HEAD_DIM)) for a in rows)
        return fk, fv, jnp.stack(logf), dk, dv, jnp.stack(state)

    return (xp, xs) + group_outputs(p_rows, p_logf, p_state) + group_outputs(s_rows, s_logf, s_state)
```

```python
import functools
import math

import jax
import jax.numpy as jnp
from jax import lax
from jax.experimental import pallas as pl
from jax.experimental.pallas import tpu as pltpu

F32 = jnp.float32
BF16 = jnp.bfloat16

CHUNK = 64
HEAD_DIM = 64
N_HEADS = 6
GROUP_W = N_HEADS * HEAD_DIM
LANES = 128
N_PAIRS = GROUP_W // LANES
DIFF_QK = HEAD_DIM // 2
ROPE_THETA = 10000.0
EPS = 1e-6
MASKED = -1e30
LOG2E = math.log2(math.e)
STRIP = 16
GROUP_ROWS = 128
LOG_G = tuple(math.log1p(-(2.0 ** (-5.0 - h))) for h in range(N_HEADS))
TOK_TILE = 512
FOX_TILE = 1024
DIFF_TILE = 512
F_ROWS = 16
VMEM_LIMIT = 56 * 1024 * 1024


def _params(*sem):
    return pltpu.CompilerParams(dimension_semantics=sem, vmem_limit_bytes=VMEM_LIMIT)


def _resident(shape):
    zeros = (0,) * len(shape)
    return pl.BlockSpec(shape, lambda *_: zeros, pipeline_mode=pl.Buffered(1))


def _sigmoid(x):
    return 1.0 / (1.0 + jnp.exp(-x))


def _log2(n):
    assert n > 0 and n & (n - 1) == 0, n
    return n.bit_length() - 1


def _dot_nt(a, b):
    return lax.dot_general(a, b, (((1,), (1,)), ((), ())), preferred_element_type=F32)


def _dot_tn(a, b):
    return lax.dot_general(a, b, (((0,), (0,)), ((), ())), preferred_element_type=F32)


def _rms_mod(x, g, scale, shift):
    y = x * lax.rsqrt(jnp.mean(x * x, axis=-1, keepdims=True) + EPS)
    return (y * g) * (1.0 + scale) + shift


def _lower_half(shape):
    return lax.broadcasted_iota(jnp.int32, shape, len(shape) - 1) < HEAD_DIM


def _half_sums(x, lo):
    a = jnp.sum(jnp.where(lo, x, 0.0), axis=-1, keepdims=True)
    b = jnp.sum(jnp.where(lo, 0.0, x), axis=-1, keepdims=True)
    return jnp.where(lo, a, b)


def _ada_kernel(c_ref, w_ref, b_ref, o_ref):
    c = c_ref[...]
    a = (c * _sigmoid(c)).astype(BF16)
    o_ref[0] = jnp.dot(a, w_ref[0].astype(BF16), preferred_element_type=F32) + b_ref[0]


def _ada(c_all, w_ada, b_ada):
    depth, d, n = w_ada.shape
    rows = c_all.shape[0]
    tn = n // 4
    return pl.pallas_call(
        _ada_kernel,
        grid=(depth, n // tn),
        in_specs=[
            pl.BlockSpec((rows, d), lambda l, j: (0, 0)),
            pl.BlockSpec((1, d, tn), lambda l, j: (l, 0, j)),
            pl.BlockSpec((1, 1, tn), lambda l, j: (l, 0, j)),
        ],
        out_specs=pl.BlockSpec((1, rows, tn), lambda l, j: (l, 0, j)),
        out_shape=jax.ShapeDtypeStruct((depth, rows, n), F32),
        compiler_params=_params("arbitrary", "arbitrary"),
        name="ada_mod",
    )(c_all, w_ada, b_ada.reshape(depth, 1, n))


def _rope(y, cos, sin_signed, half):
    width = y.shape[-1]
    lane = lax.broadcasted_iota(jnp.int32, y.shape, 1)
    ahead = pltpu.roll(y, width - half, axis=1)
    behind = pltpu.roll(y, half, axis=1)
    swapped = jnp.where((lane & half) == 0, ahead, behind)
    return y * cos + swapped * sin_signed


N_INPROJ_IN = 11


def _inproj_kernel(*refs):
    (x_ref, sc_ref, sh_ref, g_ref, wm_ref, wf_ref, bf_ref,
     c64_ref, s64_ref, c32_ref, s32_ref) = refs[:N_INPROJ_IN]
    (rq_o, rk_o, rv_o, rg_o, fq_o, fkf_o, fkb_o, fvf_o, fvb_o, lf_o,
     dq_o, dkf_o, dkb_o, dvf_o, dvb_o) = refs[-15:]
    bt, tl, d = x_ref.shape
    m = bt * tl
    h = _rms_mod(x_ref[...], g_ref[...], sc_ref[...], sh_ref[...])
    hb = h.reshape(m, d).astype(BF16)

    def piece(i):
        return jnp.dot(hb, wm_ref[:, i * GROUP_W:(i + 1) * GROUP_W], preferred_element_type=F32)

    def put(y, *refs):
        for ref in refs:
            if len(ref.shape) == 4 and ref.shape[-1] == LANES:
                for p in range(N_PAIRS):
                    yp = y[:, p * LANES:(p + 1) * LANES]
                    ref[:, p] = yp.reshape(bt, tl, LANES).astype(ref.dtype)
            else:
                ref[...] = y.reshape(ref.shape).astype(ref.dtype)

    c64, s64 = c64_ref[...], s64_ref[...]
    c32, s32 = c32_ref[...], s32_ref[...]
    put(_rope(piece(0), c64, s64, HEAD_DIM // 2), rq_o)
    put(_rope(piece(1), c64, s64, HEAD_DIM // 2), rk_o)
    put(piece(2), rv_o)
    put(piece(3), rg_o)
    put(piece(4) * (HEAD_DIM ** -0.5 * LOG2E), fq_o)
    put(piece(5), fkf_o, fkb_o)
    put(piece(6), fvf_o, fvb_o)
    z = jnp.dot(hb, wf_ref[...], preferred_element_type=F32) + bf_ref[...]
    put(jnp.minimum(z, 0.0) - jnp.log1p(jnp.exp(-jnp.abs(z))), lf_o)
    put(_rope(piece(7), c32, s32, DIFF_QK // 2) * (DIFF_QK ** -0.5 * LOG2E), dq_o)
    put(_rope(piece(8), c32, s32, DIFF_QK // 2), dkf_o, dkb_o)
    put(piece(9), dvf_o, dvb_o)


CACHE_OUTS = (5, 7, 11, 13)


def _inproj(x, scale, shift, gain, w_main, w_ff, b_ff, tabs, bt, tl, layer, depth, cache_rows):
    b, l, d = x.shape
    m = bt * tl
    grid = (b // bt, l // tl)
    tile = lambda w: pl.BlockSpec((bt, tl, w), lambda i, t: (i, t, 0))
    mod = pl.BlockSpec((bt, 1, d), lambda i, t: (i, 0, 0))
    tab = pl.BlockSpec((m, GROUP_W), lambda i, t: (t, 0))
    pairs = (pl.BlockSpec((bt, N_PAIRS, tl, LANES), lambda i, t: (i, 0, t, 0)),
             jax.ShapeDtypeStruct((b, N_PAIRS, l, LANES), BF16))
    flat = lambda w, dt: (tile(w), jax.ShapeDtypeStruct((b, l, w), dt))
    stacked = (pl.BlockSpec((1, bt, tl, GROUP_W), lambda i, t: (layer, i, t, 0)),
               jax.ShapeDtypeStruct((depth, b, l, GROUP_W), F32))
    outs = [
        flat(GROUP_W, BF16), flat(GROUP_W, BF16), flat(GROUP_W, BF16), flat(GROUP_W, F32),
        pairs, stacked, pairs, stacked, pairs,
        flat(LANES, F32),
        pairs, stacked, pairs, stacked, pairs,
    ]
    in_specs = [tile(d), mod, mod, _resident((1, d)), _resident(w_main.shape),
                _resident(w_ff.shape), _resident((1, LANES)), tab, tab, tab, tab]
    assert len(in_specs) == N_INPROJ_IN
    aliases = {}
    if cache_rows is not None:
        in_specs += [pl.BlockSpec(memory_space=pl.ANY)] * len(CACHE_OUTS)
        aliases = {N_INPROJ_IN + n: out for n, out in enumerate(CACHE_OUTS)}
    return pl.pallas_call(
        _inproj_kernel,
        grid=grid,
        in_specs=in_specs,
        out_specs=[spec for spec, _ in outs],
        out_shape=[shape for _, shape in outs],
        input_output_aliases=aliases,
        compiler_params=_params("arbitrary", "arbitrary"),
        name="in_proj",
    )(x, scale, shift, gain, w_main, w_ff, b_ff, *tabs, *(cache_rows or ()))


def _cumsum_kernel(x_ref, o_ref):
    rows, n = x_ref.shape[1], x_ref.shape[2]
    r = lax.broadcasted_iota(jnp.int32, (LANES, LANES), 0)
    c = lax.broadcasted_iota(jnp.int32, (LANES, LANES), 1)
    upper = (r <= c).astype(BF16)
    carry = jnp.zeros((rows, 1), F32)
    for j in range(n // LANES):
        x = x_ref[0, :, j * LANES:(j + 1) * LANES]
        hi = x.astype(BF16)
        r1 = x - hi.astype(F32)
        mid = r1.astype(BF16)
        lo = (r1 - mid.astype(F32)).astype(BF16)
        y = jnp.dot(jnp.concatenate([hi, mid, lo], axis=0), upper, preferred_element_type=F32)
        y = y[:rows] + y[rows:2 * rows] + y[2 * rows:]
        o_ref[0, :, j * LANES:(j + 1) * LANES] = (y + carry) * LOG2E
        carry = carry + y[:, LANES - 1:LANES]


def _cumsum(logf_t):
    b, rows, n = logf_t.shape
    spec = pl.BlockSpec((1, rows, n), lambda i: (i, 0, 0))
    return pl.pallas_call(
        _cumsum_kernel, grid=(b,), in_specs=[spec], out_specs=spec,
        out_shape=jax.ShapeDtypeStruct(logf_t.shape, F32),
        compiler_params=_params("arbitrary"),
        name="forget_cumsum",
    )(logf_t)


def _ret_kernel(q_ref, k_ref, v_ref, g_ref, s0_ref, ng_ref, nb_ref, o_ref, sn_ref, w_scr, s_scr,
                *, chunk):
    bt, tl, _ = q_ref.shape
    m = bt * tl
    t = pl.program_id(1)

    @pl.when((pl.program_id(0) == 0) & (t == 0))
    def _():
        i = lax.broadcasted_iota(jnp.int32, (m, m), 0)
        j = lax.broadcasted_iota(jnp.int32, (m, m), 1)
        visible = ((i >> _log2(tl)) == (j >> _log2(tl))) & ((j >> _log2(chunk)) <= (i >> _log2(chunk)))
        dist = jnp.abs(i - j).astype(F32)
        for h in range(N_HEADS):
            w_scr[h] = jnp.where(visible, jnp.exp(LOG_G[h] * dist), 0.0)

    @pl.when(t == 0)
    def _():
        s_scr[...] = s0_ref[...]

    head = lax.broadcasted_iota(jnp.int32, (1, GROUP_W), 1) >> _log2(HEAD_DIM)
    log_g = jnp.full((1, GROUP_W), LOG_G[N_HEADS - 1], F32)
    for h in range(N_HEADS - 1):
        log_g = jnp.where(head == h, LOG_G[h], log_g)
    pos = (lax.broadcasted_iota(jnp.int32, (m, 1), 0) & (tl - 1)).astype(F32)
    q_w = jnp.exp((pos + 1.0) * log_g)
    k_w = jnp.exp((tl - 1.0 - pos) * log_g) * HEAD_DIM ** -0.5

    qf = q_ref[...].reshape(m, GROUP_W).astype(F32)
    kf = k_ref[...].reshape(m, GROUP_W).astype(F32)
    v = v_ref[...].reshape(m, GROUP_W)
    gate = g_ref[...].reshape(m, GROUP_W)
    qwb = (qf * q_w).astype(BF16)
    kwb = (kf * k_w).astype(BF16)
    ksb = (kf * HEAD_DIM ** -0.5).astype(BF16)
    lo = _lower_half((m, LANES))
    r = lax.broadcasted_iota(jnp.int32, (LANES, LANES), 0) < HEAD_DIM
    c = lax.broadcasted_iota(jnp.int32, (LANES, LANES), 1) < HEAD_DIM
    same_head = r == c

    for p in range(N_PAIRS):
        sl = slice(p * LANES, (p + 1) * LANES)
        qp, vp = qf[:, sl], v[:, sl]
        q2 = jnp.concatenate([jnp.where(lo, qp, 0.0), jnp.where(lo, 0.0, qp)], axis=0).astype(BF16)
        s = _dot_nt(q2, ksb[:, sl])
        probs = jnp.concatenate([(s[:m] * w_scr[2 * p]).astype(BF16),
                                 (s[m:] * w_scr[2 * p + 1]).astype(BF16)], axis=0)
        o2 = jnp.dot(probs, vp, preferred_element_type=F32)
        intra = jnp.where(lo, o2[:m], o2[m:])
        tile_decay = jnp.where(r, math.exp(tl * LOG_G[2 * p]), math.exp(tl * LOG_G[2 * p + 1]))
        cross = []
        for bi in range(bt):
            rows = slice(bi * tl, (bi + 1) * tl)
            state = s_scr[bi, p]
            cross.append(jnp.dot(qwb[rows, sl], state.astype(BF16), preferred_element_type=F32))
            u = _dot_tn(kwb[rows, sl], vp[rows])
            s_scr[bi, p] = tile_decay * state + jnp.where(same_head, u, 0.0)
        y = intra + (jnp.concatenate(cross, axis=0) if bt > 1 else cross[0])
        yc = y - _half_sums(y, lo) * (1.0 / HEAD_DIM)
        var = _half_sums(yc * yc, lo) * (1.0 / HEAD_DIM)
        yn = yc * lax.rsqrt(var + EPS) * ng_ref[:, sl] + nb_ref[:, sl]
        gp = gate[:, sl]
        out = gp * _sigmoid(gp) * yn
        o_ref[:, :, sl] = out.reshape(bt, tl, LANES).astype(o_ref.dtype)

    @pl.when(t == pl.num_programs(1) - 1)
    def _():
        sn_ref[...] = s_scr[...]


def _retention(rq, rk, rv, rg, s0_pairs, norm_g, norm_b, bt, tl, chunk):
    b, l, _ = rq.shape
    m = bt * tl
    tile = pl.BlockSpec((bt, tl, GROUP_W), lambda i, t: (i, t, 0))
    st = pl.BlockSpec((bt, N_PAIRS, LANES, LANES), lambda i, t: (i, 0, 0, 0))
    return pl.pallas_call(
        functools.partial(_ret_kernel, chunk=chunk),
        grid=(b // bt, l // tl),
        in_specs=[tile, tile, tile, tile, st, _resident((1, GROUP_W)), _resident((1, GROUP_W))],
        out_specs=[tile, st],
        out_shape=[jax.ShapeDtypeStruct((b, l, GROUP_W), BF16),
                   jax.ShapeDtypeStruct(s0_pairs.shape, F32)],
        scratch_shapes=[pltpu.VMEM((N_HEADS, m, m), F32),
                        pltpu.VMEM((bt, N_PAIRS, LANES, LANES), F32)],
        compiler_params=_params("arbitrary", "arbitrary"),
        name="retention",
    )(rq, rk, rv, rg, s0_pairs, norm_g, norm_b)


def _state_to_pairs(s):
    b = s.shape[0]
    s = s.reshape(b, N_PAIRS, 2, HEAD_DIM, HEAD_DIM)
    z = jnp.zeros_like(s[:, :, 0])
    top = jnp.concatenate([s[:, :, 0], z], axis=-1)
    bot = jnp.concatenate([z, s[:, :, 1]], axis=-1)
    return jnp.concatenate([top, bot], axis=-2)


def _pairs_to_state(sp):
    b = sp.shape[0]
    a = sp[:, :, :HEAD_DIM, :HEAD_DIM]
    d = sp[:, :, HEAD_DIM:, HEAD_DIM:]
    return jnp.stack([a, d], axis=2).reshape(b, N_HEADS, HEAD_DIM, HEAD_DIM)


def _softmax_step(s, m, l):
    m_new = jnp.maximum(m, jnp.max(s, axis=-1, keepdims=True))
    alpha = jnp.exp2(m - m_new)
    p = jnp.exp2(s - m_new)
    return p, m_new, alpha * l + jnp.sum(p, axis=-1, keepdims=True), alpha


def _stack_masked(qp, n_streams):
    width = LANES // n_streams
    owner = lax.broadcasted_iota(jnp.int32, qp.shape, 1) >> _log2(width)
    return jnp.concatenate([jnp.where(owner == i, qp, 0.0) for i in range(n_streams)],
                           axis=0).astype(BF16)


def _pair_block(qp, kp, vp, biases, mask, state, frames_last=False):
    n, tq = len(state), qp.shape[0]
    q_stack = _stack_masked(qp, n)
    s = jnp.dot(q_stack, kp, preferred_element_type=F32) if frames_last else _dot_nt(q_stack, kp)
    probs, new = [], []
    for i in range(n):
        si = s[i * tq:(i + 1) * tq]
        if biases is not None:
            si = si + biases[i]
        if mask is not None:
            si = jnp.where(mask, si, MASKED)
        p, m_new, l_new, alpha = _softmax_step(si, state[i][0], state[i][1])
        probs.append(p.astype(BF16))
        new.append((m_new, l_new, alpha))
    probs = jnp.concatenate(probs, axis=0)
    pv = _dot_nt(probs, vp) if frames_last else jnp.dot(probs, vp, preferred_element_type=F32)
    return [(m_new, l_new, alpha * state[i][2] + pv[i * tq:(i + 1) * tq])
            for i, (m_new, l_new, alpha) in enumerate(new)]


def _cached_pair(ref, p):
    pair = ref[0, 0, 2 * p:2 * p + 2]
    return pair.reshape(LANES, pair.shape[-1]).astype(BF16)


def _fresh(n, tq):
    return [(jnp.full((tq, 1), MASKED, F32), jnp.zeros((tq, 1), F32), jnp.zeros((tq, LANES), F32))
            for _ in range(n)]


def _fox_out(state, lo):
    (_, la, acca), (_, lb, accb) = state
    return jnp.where(lo, acca / la, accb / lb)


def _lam(lam_ref, lam_init):
    a = lam_ref[...]
    t1 = jnp.sum(a[0:1] * a[1:2], axis=-1, keepdims=True)
    t2 = jnp.sum(a[2:3] * a[3:4], axis=-1, keepdims=True)
    return jnp.exp(t1) - jnp.exp(t2) + lam_init


def _diff_norm(o, lo, gain, lam_init):
    ms = _half_sums(o * o, lo) * (1.0 / HEAD_DIM)
    return (o * lax.rsqrt(ms + EPS) * gain) * (1.0 - lam_init)


def _diff_out(state, lo, lam, gain, lam_init):
    (_, l0, a0), (_, l1, a1), (_, l2, a2), (_, l3, a3) = state
    o = jnp.where(lo, a0 / l0, a2 / l2) - lam * jnp.where(lo, a1 / l1, a3 / l3)
    return _diff_norm(o, lo, gain, lam_init)


def _rows(r0, n):
    return slice(r0, r0 + n)


def _tile_update(scr, q_scr, k, v, tq, key_bias_fn, row_bias_scr, visibility):
    s_scr, p_scr, m_scr, c_scr, a_scr, acc_scr = scr
    n_rows, tk = s_scr.shape
    n_chunks = tk // LANES
    chunk = lambda j: slice(j * LANES, (j + 1) * LANES)
    for r0 in range(0, n_rows, GROUP_ROWS):
        rows = _rows(r0, GROUP_ROWS)
        s_scr[rows, :] = _dot_nt(q_scr[rows, :], k)
    for r0 in range(0, n_rows, STRIP):
        rows = _rows(r0, STRIP)
        stream, row_in_stream = divmod(r0, tq)
        live = []
        for j in range(n_chunks):
            vis = visibility(row_in_stream, j)
            if vis is None:
                continue
            t = s_scr[rows, chunk(j)]
            if key_bias_fn is not None:
                t = t - key_bias_fn(stream, j)
            if vis is not True:
                t = jnp.where(vis, t, MASKED)
            if key_bias_fn is not None or vis is not True:
                s_scr[rows, chunk(j)] = t
            live.append(t)
        top = functools.reduce(jnp.maximum, live)
        top = jnp.broadcast_to(jnp.max(top, axis=-1, keepdims=True), (STRIP, LANES))
        m_old = m_scr[rows, :]
        if row_bias_scr is not None:
            row_bias = row_bias_scr[rows, :]
            m_new = jnp.maximum(m_old, top + row_bias)
            c_scr[rows, :] = m_new - row_bias
        else:
            m_new = jnp.maximum(m_old, top)
            c_scr[rows, :] = m_new
        a_scr[rows, :] = jnp.exp2(m_old - m_new)
        m_scr[rows, :] = m_new
    for r0 in range(0, n_rows, STRIP):
        rows = _rows(r0, STRIP)
        centre = c_scr[rows, :]
        for j in range(n_chunks):
            if visibility(r0 % tq, j) is None:
                p_scr[rows, chunk(j)] = jnp.zeros((STRIP, LANES), BF16)
            else:
                p_scr[rows, chunk(j)] = jnp.exp2((s_scr[rows, chunk(j)] - centre).astype(BF16))
    ones_col = (lax.broadcasted_iota(jnp.int32, v.shape, 1) == 0).astype(BF16)
    v_ext = jnp.concatenate([v, ones_col], axis=1)
    for r0 in range(0, n_rows, GROUP_ROWS):
        rows = _rows(r0, GROUP_ROWS)
        alpha = a_scr[rows, :]
        acc_scr[rows, :] = (jnp.concatenate([alpha, alpha], axis=1) * acc_scr[rows, :]
                            + jnp.dot(p_scr[rows, :], v_ext, preferred_element_type=F32))


def _all_visible(row0, j):
    return True


def _causal_visibility(row0, j):
    if j * LANES > row0 + STRIP - 1:
        return None
    if j * LANES + LANES - 1 <= row0:
        return True
    rel = (lax.broadcasted_iota(jnp.int32, (STRIP, LANES), 1)
           - lax.broadcasted_iota(jnp.int32, (STRIP, LANES), 0))
    return rel <= row0 - j * LANES


def _chunk_visibility(row0, j):
    q_chunk = row0 // CHUNK
    first, last = j * (LANES // CHUNK), (j + 1) * (LANES // CHUNK) - 1
    if first > q_chunk:
        return None
    if last <= q_chunk:
        return True
    return (lax.broadcasted_iota(jnp.int32, (STRIP, LANES), 1) >> _log2(CHUNK)) <= q_chunk - first


def _init_att_state(m_scr, acc_scr):
    m_scr[...] = jnp.full(m_scr.shape, MASKED, F32)
    acc_scr[...] = jnp.zeros(acc_scr.shape, F32)


def _normalised(acc_scr):
    acc = acc_scr[...]
    return acc[:, :LANES] / acc[:, LANES:LANES + 1]


def _fox_kernel(q_ref, k_ref, v_ref, fq_ref, fk_ref, o_ref,
                q_scr, fq_scr, m_scr, c_scr, a_scr, acc_scr, s_scr, p_scr):
    tq = q_ref.shape[2]
    tk = s_scr.shape[1]
    qi = pl.program_id(1)
    scr = (s_scr, p_scr, m_scr, c_scr, a_scr, acc_scr)
    chunks = tk // LANES
    head_col = lax.broadcasted_iota(jnp.int32, fq_ref.shape[1:], 1)
    lo = _lower_half((tq, LANES))

    def pair(p, carry):
        q_scr[...] = _stack_masked(q_ref[0, p].astype(F32), 2)
        for half in range(2):
            col = jnp.sum(jnp.where(head_col == 2 * p + half, fq_ref[0], 0.0), axis=-1, keepdims=True)
            fq_scr[half * tq:(half + 1) * tq, :] = jnp.broadcast_to(col, (tq, LANES))
        _init_att_state(m_scr, acc_scr)

        def tile(ki, visibility):
            keys = pl.ds(pl.multiple_of(ki * tk, tk), tk)
            key_bias = lambda stream, j: fk_ref[0, 2 * p + stream, ki * chunks + j]
            _tile_update(scr, q_scr, k_ref[0, p, keys, :], v_ref[0, p, keys, :], tq,
                         key_bias, fq_scr, visibility)

        def full_tile(ki, c):
            tile(ki, _all_visible)
            return c

        lax.fori_loop(0, qi, full_tile, 0)
        tile(qi, _causal_visibility)
        o = _normalised(acc_scr)
        o_ref[0, p] = jnp.where(lo, o[:tq], o[tq:]).astype(o_ref.dtype)
        return carry

    lax.fori_loop(0, N_PAIRS, pair, 0)


def _diff_kernel(q_ref, k_ref, v_ref, lam_ref, g_ref, o_ref,
                 q_scr, m_scr, c_scr, a_scr, acc_scr, s_scr, p_scr, *, lam_init):
    tq = q_ref.shape[2]
    tk = s_scr.shape[1]
    qi = pl.program_id(1)
    scr = (s_scr, p_scr, m_scr, c_scr, a_scr, acc_scr)
    lo = _lower_half((tq, LANES))
    lam = _lam(lam_ref, lam_init)

    def pair(p, carry):
        q_scr[...] = _stack_masked(q_ref[0, p].astype(F32), 4)
        _init_att_state(m_scr, acc_scr)

        def tile(ki, visibility):
            keys = pl.ds(pl.multiple_of(ki * tk, tk), tk)
            _tile_update(scr, q_scr, k_ref[0, p, keys, :], v_ref[0, p, keys, :], tq,
                         None, None, visibility)

        def full_tile(ki, c):
            tile(ki, _all_visible)
            return c

        lax.fori_loop(0, qi, full_tile, 0)
        tile(qi, _chunk_visibility)
        o = _normalised(acc_scr)
        o = (jnp.where(lo, o[:tq], o[2 * tq:3 * tq]) - lam * jnp.where(lo, o[tq:2 * tq], o[3 * tq:]))
        o_ref[0, p] = _diff_norm(o, lo, g_ref[p], lam_init).astype(o_ref.dtype)
        return carry

    lax.fori_loop(0, N_PAIRS, pair, 0)


def _prompt_att_call(kernel_fn, name, n_streams, tile, q, k, v, extra, extra_specs, row_bias):
    b, _, l, _ = q.shape
    tq = tk = tile
    assert l % tq == 0 and tq % CHUNK == 0
    rows = n_streams * tq
    qspec = pl.BlockSpec((1, N_PAIRS, tq, LANES), lambda i, t: (i, 0, t, 0))
    kspec = pl.BlockSpec((1, N_PAIRS, l, LANES), lambda i, t: (i, 0, 0, 0), pipeline_mode=pl.Buffered(1))
    stats = pltpu.VMEM((rows, LANES), F32)
    return pl.pallas_call(
        kernel_fn, grid=(b, l // tq),
        in_specs=[qspec, kspec, kspec] + extra_specs,
        out_specs=qspec,
        out_shape=jax.ShapeDtypeStruct(q.shape, BF16),
        scratch_shapes=[pltpu.VMEM((rows, LANES), BF16)] + ([stats] if row_bias else [])
        + [stats, stats, stats, pltpu.VMEM((rows, 2 * LANES), F32),
           pltpu.VMEM((rows, tk), F32), pltpu.VMEM((rows, tk), BF16)],
        compiler_params=_params("arbitrary", "arbitrary"),
        name=name,
    )(q, k, v, *extra)


def _fox_prompt(q, k, v, fq, fk_rep):
    tq = min(FOX_TILE, q.shape[2])
    specs = [pl.BlockSpec((1, tq, fq.shape[-1]), lambda i, t: (i, t, 0)),
             pl.BlockSpec((1,) + fk_rep.shape[1:], lambda i, t: (i, 0, 0, 0, 0),
                          pipeline_mode=pl.Buffered(1))]
    return _prompt_att_call(_fox_kernel, "fox_attention", 2, tq, q, k, v, (fq, fk_rep), specs, True)


def _diff_prompt(q, k, v, lam_rows, gain, lam_init):
    gain_pairs = gain.reshape(N_PAIRS, 1, LANES)
    specs = [_resident(lam_rows.shape), _resident(gain_pairs.shape)]
    return _prompt_att_call(functools.partial(_diff_kernel, lam_init=lam_init), "diff_attention", 4,
                            min(DIFF_TILE, q.shape[2]), q, k, v, (lam_rows, gain_pairs), specs, False)


def _fox_sample_kernel(q_ref, kp_ref, vp_ref, kn_ref, vn_ref, fq_ref, fkp_ref, fkn_ref, o_ref):
    tq, tn = q_ref.shape[2], kn_ref.shape[2]
    fq, fkp, fkn = fq_ref[0], fkp_ref[0], fkn_ref[0]
    causal = lax.broadcasted_iota(jnp.int32, (tq, tn), 1) <= lax.broadcasted_iota(jnp.int32, (tq, tn), 0)
    lo = _lower_half((tq, LANES))
    for p in range(N_PAIRS):
        idx = (2 * p, 2 * p + 1)
        qf = q_ref[0, p].astype(F32)
        state = _pair_block(qf, _cached_pair(kp_ref, p), _cached_pair(vp_ref, p),
                            [fq[:, h:h + 1] - fkp[h:h + 1, :] for h in idx], None, _fresh(2, tq),
                            frames_last=True)
        state = _pair_block(qf, kn_ref[0, p], vn_ref[0, p],
                            [fq[:, h:h + 1] - fkn[h:h + 1, :] for h in idx], causal, state)
        o_ref[0, p] = _fox_out(state, lo).astype(o_ref.dtype)


def _diff_sample_kernel(q_ref, kp_ref, vp_ref, kn_ref, vn_ref, lam_ref, g_ref, o_ref,
                        *, lam_init, past_len, n_new):
    tq, tn = q_ref.shape[2], kn_ref.shape[2]
    col = lax.broadcasted_iota(jnp.int32, (tq, tn), 1)
    row = lax.broadcasted_iota(jnp.int32, (tq, tn), 0)
    shift = _log2(CHUNK)
    visible = (((past_len + col) >> shift) <= ((past_len + row) >> shift)) & (col < n_new)
    lo = _lower_half((tq, LANES))
    lam = _lam(lam_ref, lam_init)
    for p in range(N_PAIRS):
        sl = slice(p * LANES, (p + 1) * LANES)
        qf = q_ref[0, p].astype(F32)
        state = _pair_block(qf, _cached_pair(kp_ref, p), _cached_pair(vp_ref, p), None, None,
                            _fresh(4, tq), frames_last=True)
        state = _pair_block(qf, kn_ref[0, p], vn_ref[0, p], None, visible, state)
        o_ref[0, p] = _diff_out(state, lo, lam, g_ref[:, sl], lam_init).astype(o_ref.dtype)


def _sample_specs(q, k_past, k_new, layer):
    b = q.shape[0]
    pairs = lambda a: pl.BlockSpec((1,) + a.shape[1:], lambda i: (i, 0, 0, 0))
    past = pl.BlockSpec((1, 1) + k_past.shape[2:], lambda i: (layer, i, 0, 0, 0))
    return b, pairs(q), past, pairs(k_new)


def _fox_sample(q, k_past, v_past, k_new, v_new, fq, fk_past, fk_new, layer):
    b, qspec, pspec, nspec = _sample_specs(q, k_past, k_new, layer)
    frow = lambda a: pl.BlockSpec((1,) + a.shape[1:], lambda i: (i, 0, 0))
    return pl.pallas_call(
        _fox_sample_kernel, grid=(b,),
        in_specs=[qspec, pspec, pspec, nspec, nspec, frow(fq), frow(fk_past), frow(fk_new)],
        out_specs=qspec,
        out_shape=jax.ShapeDtypeStruct(q.shape, BF16),
        compiler_params=_params("arbitrary"),
        name="fox_attention_cached",
    )(q, k_past, v_past, k_new, v_new, fq, fk_past, fk_new)


def _diff_sample(q, k_past, v_past, k_new, v_new, lam_rows, gain, lam_init, n_new, layer):
    b, qspec, pspec, nspec = _sample_specs(q, k_past, k_new, layer)
    past_len = k_past.shape[-1]
    assert past_len % CHUNK == 0
    return pl.pallas_call(
        functools.partial(_diff_sample_kernel, lam_init=lam_init, past_len=past_len, n_new=n_new),
        grid=(b,),
        in_specs=[qspec, pspec, pspec, nspec, nspec, _resident(lam_rows.shape), _resident((1, GROUP_W))],
        out_specs=qspec,
        out_shape=jax.ShapeDtypeStruct(q.shape, BF16),
        compiler_params=_params("arbitrary"),
        name="diff_attention_cached",
    )(q, k_past, v_past, k_new, v_new, lam_rows, gain)


def _mlp_kernel(x_ref, r_ref, f_ref, d_ref, g1_ref, sc_ref, sh_ref, g2_ref, ng_ref,
                wo_ref, wu_ref, wd_ref, fg_ref, o_ref, *, final_norm):
    bt, tl, d = x_ref.shape
    m = bt * tl
    mix = jnp.dot(r_ref[...].reshape(m, GROUP_W), wo_ref[0], preferred_element_type=F32)
    for i, ref in ((1, f_ref), (2, d_ref)):
        for p in range(N_PAIRS):
            mix = mix + jnp.dot(ref[:, p].reshape(m, LANES), wo_ref[i, p * LANES:(p + 1) * LANES, :],
                                preferred_element_type=F32)
    x = x_ref[...] + g1_ref[...] * mix.reshape(bt, tl, d)
    hb = _rms_mod(x, ng_ref[...], sc_ref[...], sh_ref[...]).reshape(m, d).astype(BF16)
    d_ff = wu_ref.shape[1]
    step = d_ff // 4
    y = jnp.zeros((m, d), F32)
    for j in range(0, d_ff, step):
        u = jnp.maximum(jnp.dot(hb, wu_ref[:, j:j + step], preferred_element_type=F32), 0.0)
        y = y + jnp.dot((u * u).astype(BF16), wd_ref[j:j + step, :], preferred_element_type=F32)
    x = x + g2_ref[...] * y.reshape(bt, tl, d)
    if final_norm:
        x = x * lax.rsqrt(jnp.mean(x * x, axis=-1, keepdims=True) + EPS) * fg_ref[...]
    o_ref[...] = x


def _mlp(x, r, f, dd, g1, sc2, sh2, g2, norm_g, w_out3, w_up, w_down, final_g, bt, tl, final_norm):
    b, l, d = x.shape
    tile = lambda w: pl.BlockSpec((bt, tl, w), lambda i, t: (i, t, 0))
    pairs = pl.BlockSpec((bt, N_PAIRS, tl, LANES), lambda i, t: (i, 0, t, 0))
    mod = pl.BlockSpec((bt, 1, d), lambda i, t: (i, 0, 0))
    return pl.pallas_call(
        functools.partial(_mlp_kernel, final_norm=final_norm),
        grid=(b // bt, l // tl),
        in_specs=[tile(d), tile(GROUP_W), pairs, pairs, mod, mod, mod, mod,
                  _resident((1, d)), _resident(w_out3.shape), _resident(w_up.shape),
                  _resident(w_down.shape), _resident((1, d))],
        out_specs=tile(d),
        out_shape=jax.ShapeDtypeStruct(x.shape, F32),
        compiler_params=_params("arbitrary", "arbitrary"),
        name="out_proj_mlp",
    )(x, r, f, dd, g1, sc2, sh2, g2, norm_g, w_out3, w_up, w_down, final_g)


def _rope_tables(pos, half, batch_reps):
    inv = ROPE_THETA ** (-jnp.arange(half, dtype=F32) / half)
    ang = pos.astype(F32)[:, None] * inv[None, :]
    cos, sin = jnp.cos(ang), jnp.sin(ang)
    reps = GROUP_W // (2 * half)
    cos = jnp.tile(jnp.concatenate([cos, cos], axis=-1), (batch_reps, reps))
    sin = jnp.tile(jnp.concatenate([-sin, sin], axis=-1), (batch_reps, reps))
    return cos, sin


def _pad_axis(a, axis, size):
    pad = [(0, 0)] * a.ndim
    pad[axis] = (0, size - a.shape[axis])
    return jnp.pad(a, pad)


def _forget_cumsum(logf_all):
    lk = logf_all.shape[1]
    x = jnp.transpose(logf_all, (0, 2, 1))
    x = _pad_axis(_pad_axis(x, 1, F_ROWS), 2, -(-lk // LANES) * LANES)
    return _cumsum(x)


def _group_layer(x, mods, tabs, past, lw, lam_init, bt, tl, final_g, final_norm, layer, depth,
                 cache_rows):
    b, l, d = x.shape
    sh1, sc1, g1, sh2, sc2, g2 = mods
    (rq, rk, rv, rg, fq, fk_f, fk_b, fv_f, fv_b, logf_pad,
     dq, dk_f, dk_b, dv_f, dv_b) = _inproj(x, sc1, sh1, lw["norm1_g"], lw["w_main"], lw["w_ff"],
                                           lw["b_ff"], tabs, bt, tl, layer, depth, cache_rows)
    logf = logf_pad[:, :, :N_HEADS]
    if past is None:
        s0 = jnp.zeros((b, N_HEADS, HEAD_DIM, HEAD_DIM), F32)
        f_t = _forget_cumsum(logf)
        fq_cum = jnp.transpose(f_t[:, :N_HEADS + 2, :l], (0, 2, 1))
        fk_rep = jnp.broadcast_to(f_t[:, :N_HEADS].reshape(b, N_HEADS, l // LANES, 1, LANES),
                                  (b, N_HEADS, l // LANES, STRIP, LANES))
        f_out = _fox_prompt(fq, fk_b, fv_b, fq_cum, fk_rep)
        d_out = _diff_prompt(dq, dk_b, dv_b, lw["lam_rows"], lw["diff_norm_g"], lam_init)
    else:
        pfk, pfv, plogf, pdk, pdv, s0 = past
        past_len = pfk.shape[-1]
        f_t = _forget_cumsum(jnp.concatenate([plogf, logf], axis=1))
        fq_cum = jnp.transpose(f_t[:, :N_HEADS + 2, past_len:past_len + l], (0, 2, 1))
        new_rows = lambda a: _pad_axis(a, 2, LANES)
        f_out = _fox_sample(fq, pfk, pfv, new_rows(fk_b), new_rows(fv_b), fq_cum,
                            f_t[:, :, :past_len], f_t[:, :, past_len:past_len + LANES], layer)
        d_out = _diff_sample(dq, pdk, pdv, new_rows(dk_b), new_rows(dv_b), lw["lam_rows"],
                             lw["diff_norm_g"], lam_init, l, layer)
    r_out, s_pairs = _retention(rq, rk, rv, rg, _state_to_pairs(s0), lw["ret_norm_g"],
                                lw["ret_norm_b"], bt, tl, min(CHUNK, l))
    x = _mlp(x, r_out, f_out, d_out, g1, sc2, sh2, g2, lw["norm2_g"], lw["w_out3"], lw["w_up"],
             lw["w_down"], final_g, bt, tl, final_norm)
    return x, (fk_f, fv_f, dk_f, dv_f), logf, _pairs_to_state(s_pairs)


def kernel(x_prompt, x_sample, cache_fox_k, cache_fox_v, cache_fox_logf, cache_diff_k, cache_diff_v,
           state_ret, c_prompt, c_sample, norm1_g, norm2_g, w_ada, b_ada, w_in, b_forget,
           ret_norm_g, ret_norm_b, lam_q1, lam_k1, lam_q2, lam_k2, diff_norm_g, w_out, w_up, w_down,
           final_g):
    depth, d = norm1_g.shape
    bp, lp, _ = x_prompt.shape
    bs, ls, _ = x_sample.shape
    past_len = cache_fox_k.shape[2]

    rows = -(-(bp + bs) // 8) * 8
    c_all = _pad_axis(jnp.concatenate([c_prompt, c_sample], axis=0), 0, rows)
    mod = _ada(c_all, w_ada, b_ada)

    def mods(li, r0, n):
        return [mod[li, r0:r0 + n, i * d:(i + 1) * d].reshape(n, 1, d) for i in range(6)]

    ff0 = 7 * GROUP_W
    w_main = jnp.concatenate([w_in[:, :, :ff0], w_in[:, :, ff0 + N_HEADS:]], axis=-1).astype(BF16)
    w_ff = _pad_axis(w_in[:, :, ff0:ff0 + N_HEADS], 2, LANES).astype(BF16)
    b_ff = _pad_axis(b_forget, 1, LANES).reshape(depth, 1, LANES)
    w_out3 = w_out.reshape(depth, 3, GROUP_W, d).astype(BF16)
    w_up_b, w_down_b = w_up.astype(BF16), w_down.astype(BF16)
    lam_rows = _pad_axis(_pad_axis(jnp.stack([lam_q1, lam_k1, lam_q2, lam_k2], axis=1), 2, LANES), 1, 8)
    final_row = final_g.reshape(1, d)

    tl_p = min(TOK_TILE, lp)
    p_pos = jnp.arange(lp, dtype=jnp.int32)
    s_pos = past_len + jnp.arange(ls, dtype=jnp.int32)
    tabs_p = _rope_tables(p_pos, HEAD_DIM // 2, 1) + _rope_tables(p_pos, DIFF_QK // 2, 1)
    tabs_s = _rope_tables(s_pos, HEAD_DIM // 2, bs) + _rope_tables(s_pos, DIFF_QK // 2, bs)

    frames_last = lambda a: jnp.transpose(a, (0, 1, 3, 4, 2))
    fox_k_t, fox_v_t = frames_last(cache_fox_k), frames_last(cache_fox_v)
    diff_k_t, diff_v_t = frames_last(cache_diff_k), frames_last(cache_diff_v)

    xp, xs = x_prompt, x_sample
    p_rows, s_rows = None, None
    p_logf, p_state, s_logf, s_state = [], [], [], []
    for li in range(depth):
        lam_init = 0.8 - 0.6 * math.exp(-0.3 * li)
        lw = dict(norm1_g=norm1_g[li].reshape(1, d), norm2_g=norm2_g[li].reshape(1, d),
                  w_main=w_main[li], w_ff=w_ff[li], b_ff=b_ff[li],
                  ret_norm_g=ret_norm_g[li].reshape(1, GROUP_W), ret_norm_b=ret_norm_b[li].reshape(1, GROUP_W),
                  lam_rows=lam_rows[li], diff_norm_g=diff_norm_g[li].reshape(1, GROUP_W),
                  w_out3=w_out3[li], w_up=w_up_b[li], w_down=w_down_b[li])
        last = li == depth - 1
        xp, p_rows, logf, state = _group_layer(xp, mods(li, 0, bp), tabs_p, None, lw, lam_init, 1, tl_p,
                                               final_row, last, li, depth, p_rows)
        p_logf.append(logf)
        p_state.append(state)
        past = (fox_k_t, fox_v_t, cache_fox_logf[li], diff_k_t, diff_v_t, state_ret[li])
        xs, s_rows, logf, state = _group_layer(xs, mods(li, bp, bs), tabs_s, past, lw, lam_init, bs, ls,
                                               final_row, last, li, depth, s_rows)
        s_logf.append(logf)
        s_state.append(state)

    def group_outputs(rows, logf, state):
        fk, fv, dk, dv = (a.reshape(a.shape[:3] + (N_HEADS, HEAD_DIM)) for a in rows)
        return fk, fv, jnp.stack(logf), dk, dv, jnp.stack(state)

    return (xp, xs) + group_outputs(p_rows, p_logf, p_state) + group_outputs(s_rows, s_logf, s_state)
```

```python
import functools
import math

import jax
import jax.numpy as jnp
from jax import lax
from jax.experimental import pallas as pl
from jax.experimental.pallas import tpu as pltpu

F32 = jnp.float32
BF16 = jnp.bfloat16

CHUNK = 64
HEAD_DIM = 64
N_HEADS = 6
GROUP_W = N_HEADS * HEAD_DIM
LANES = 128
N_PAIRS = GROUP_W // LANES
DIFF_QK = HEAD_DIM // 2
ROPE_THETA = 10000.0
EPS = 1e-6
MASKED = -1e30
LOG2E = math.log2(math.e)
STRIP = 16
GROUP_ROWS = 128
LOG_G = tuple(math.log1p(-(2.0 ** (-5.0 - h))) for h in range(N_HEADS))
TOK_TILE = 512
FOX_TILE = 1024
DIFF_TILE = 1024
F_ROWS = 16
VMEM_LIMIT = 56 * 1024 * 1024


def _params(*sem):
    return pltpu.CompilerParams(dimension_semantics=sem, vmem_limit_bytes=VMEM_LIMIT)


def _resident(shape):
    zeros = (0,) * len(shape)
    return pl.BlockSpec(shape, lambda *_: zeros, pipeline_mode=pl.Buffered(1))


def _sigmoid(x):
    return 1.0 / (1.0 + jnp.exp(-x))


def _log2(n):
    assert n > 0 and n & (n - 1) == 0, n
    return n.bit_length() - 1


def _dot_nt(a, b):
    return lax.dot_general(a, b, (((1,), (1,)), ((), ())), preferred_element_type=F32)


def _dot_tn(a, b):
    return lax.dot_general(a, b, (((0,), (0,)), ((), ())), preferred_element_type=F32)


def _rms_mod(x, g, scale, shift):
    y = x * lax.rsqrt(jnp.mean(x * x, axis=-1, keepdims=True) + EPS)
    return (y * g) * (1.0 + scale) + shift


def _lower_half(shape):
    return lax.broadcasted_iota(jnp.int32, shape, len(shape) - 1) < HEAD_DIM


def _half_sums(x, lo):
    a = jnp.sum(jnp.where(lo, x, 0.0), axis=-1, keepdims=True)
    b = jnp.sum(jnp.where(lo, 0.0, x), axis=-1, keepdims=True)
    return jnp.where(lo, a, b)


def _ada_kernel(c_ref, w_ref, b_ref, o_ref):
    c = c_ref[...]
    a = (c * _sigmoid(c)).astype(BF16)
    o_ref[0] = jnp.dot(a, w_ref[0].astype(BF16), preferred_element_type=F32) + b_ref[0]


def _ada(c_all, w_ada, b_ada):
    depth, d, n = w_ada.shape
    rows = c_all.shape[0]
    tn = n // 4
    return pl.pallas_call(
        _ada_kernel,
        grid=(depth, n // tn),
        in_specs=[
            pl.BlockSpec((rows, d), lambda l, j: (0, 0)),
            pl.BlockSpec((1, d, tn), lambda l, j: (l, 0, j)),
            pl.BlockSpec((1, 1, tn), lambda l, j: (l, 0, j)),
        ],
        out_specs=pl.BlockSpec((1, rows, tn), lambda l, j: (l, 0, j)),
        out_shape=jax.ShapeDtypeStruct((depth, rows, n), F32),
        compiler_params=_params("arbitrary", "arbitrary"),
        name="ada_mod",
    )(c_all, w_ada, b_ada.reshape(depth, 1, n))


def _rope(y, cos, sin_signed, half):
    width = y.shape[-1]
    lane = lax.broadcasted_iota(jnp.int32, y.shape, 1)
    ahead = pltpu.roll(y, width - half, axis=1)
    behind = pltpu.roll(y, half, axis=1)
    swapped = jnp.where((lane & half) == 0, ahead, behind)
    return y * cos + swapped * sin_signed


N_INPROJ_IN = 11


def _inproj_kernel(*refs):
    (x_ref, sc_ref, sh_ref, g_ref, wm_ref, wf_ref, bf_ref,
     c64_ref, s64_ref, c32_ref, s32_ref) = refs[:N_INPROJ_IN]
    (rq_o, rk_o, rv_o, rg_o, fq_o, fkf_o, fkb_o, fvf_o, fvb_o, lf_o,
     dq_o, dkf_o, dkb_o, dvf_o, dvb_o) = refs[-15:]
    bt, tl, d = x_ref.shape
    m = bt * tl
    h = _rms_mod(x_ref[...], g_ref[...], sc_ref[...], sh_ref[...])
    hb = h.reshape(m, d).astype(BF16)

    def piece(i):
        return jnp.dot(hb, wm_ref[:, i * GROUP_W:(i + 1) * GROUP_W], preferred_element_type=F32)

    def put(y, *refs):
        for ref in refs:
            if len(ref.shape) == 4 and ref.shape[-1] == LANES:
                for p in range(N_PAIRS):
                    yp = y[:, p * LANES:(p + 1) * LANES]
                    ref[:, p] = yp.reshape(bt, tl, LANES).astype(ref.dtype)
            else:
                ref[...] = y.reshape(ref.shape).astype(ref.dtype)

    c64, s64 = c64_ref[...], s64_ref[...]
    c32, s32 = c32_ref[...], s32_ref[...]
    put(_rope(piece(0), c64, s64, HEAD_DIM // 2), rq_o)
    put(_rope(piece(1), c64, s64, HEAD_DIM // 2), rk_o)
    put(piece(2), rv_o)
    put(piece(3), rg_o)
    put(piece(4) * (HEAD_DIM ** -0.5 * LOG2E), fq_o)
    put(piece(5), fkf_o, fkb_o)
    put(piece(6), fvf_o, fvb_o)
    z = jnp.dot(hb, wf_ref[...], preferred_element_type=F32) + bf_ref[...]
    put(jnp.minimum(z, 0.0) - jnp.log1p(jnp.exp(-jnp.abs(z))), lf_o)
    put(_rope(piece(7), c32, s32, DIFF_QK // 2) * (DIFF_QK ** -0.5 * LOG2E), dq_o)
    put(_rope(piece(8), c32, s32, DIFF_QK // 2), dkf_o, dkb_o)
    put(piece(9), dvf_o, dvb_o)


CACHE_OUTS = (5, 7, 11, 13)


def _inproj(x, scale, shift, gain, w_main, w_ff, b_ff, tabs, bt, tl, layer, depth, cache_rows):
    b, l, d = x.shape
    m = bt * tl
    grid = (b // bt, l // tl)
    tile = lambda w: pl.BlockSpec((bt, tl, w), lambda i, t: (i, t, 0))
    mod = pl.BlockSpec((bt, 1, d), lambda i, t: (i, 0, 0))
    tab = pl.BlockSpec((m, GROUP_W), lambda i, t: (t, 0))
    pairs = (pl.BlockSpec((bt, N_PAIRS, tl, LANES), lambda i, t: (i, 0, t, 0)),
             jax.ShapeDtypeStruct((b, N_PAIRS, l, LANES), BF16))
    flat = lambda w, dt: (tile(w), jax.ShapeDtypeStruct((b, l, w), dt))
    stacked = (pl.BlockSpec((1, bt, tl, GROUP_W), lambda i, t: (layer, i, t, 0)),
               jax.ShapeDtypeStruct((depth, b, l, GROUP_W), F32))
    outs = [
        flat(GROUP_W, BF16), flat(GROUP_W, BF16), flat(GROUP_W, BF16), flat(GROUP_W, F32),
        pairs, stacked, pairs, stacked, pairs,
        flat(LANES, F32),
        pairs, stacked, pairs, stacked, pairs,
    ]
    in_specs = [tile(d), mod, mod, _resident((1, d)), _resident(w_main.shape),
                _resident(w_ff.shape), _resident((1, LANES)), tab, tab, tab, tab]
    assert len(in_specs) == N_INPROJ_IN
    aliases = {}
    if cache_rows is not None:
        in_specs += [pl.BlockSpec(memory_space=pl.ANY)] * len(CACHE_OUTS)
        aliases = {N_INPROJ_IN + n: out for n, out in enumerate(CACHE_OUTS)}
    return pl.pallas_call(
        _inproj_kernel,
        grid=grid,
        in_specs=in_specs,
        out_specs=[spec for spec, _ in outs],
        out_shape=[shape for _, shape in outs],
        input_output_aliases=aliases,
        compiler_params=_params("arbitrary", "arbitrary"),
        name="in_proj",
    )(x, scale, shift, gain, w_main, w_ff, b_ff, *tabs, *(cache_rows or ()))


def _cumsum_kernel(x_ref, o_ref):
    rows, n = x_ref.shape[1], x_ref.shape[2]
    r = lax.broadcasted_iota(jnp.int32, (LANES, LANES), 0)
    c = lax.broadcasted_iota(jnp.int32, (LANES, LANES), 1)
    upper = (r <= c).astype(BF16)
    carry = jnp.zeros((rows, 1), F32)
    for j in range(n // LANES):
        x = x_ref[0, :, j * LANES:(j + 1) * LANES]
        hi = x.astype(BF16)
        r1 = x - hi.astype(F32)
        mid = r1.astype(BF16)
        lo = (r1 - mid.astype(F32)).astype(BF16)
        y = jnp.dot(jnp.concatenate([hi, mid, lo], axis=0), upper, preferred_element_type=F32)
        y = y[:rows] + y[rows:2 * rows] + y[2 * rows:]
        o_ref[0, :, j * LANES:(j + 1) * LANES] = (y + carry) * LOG2E
        carry = carry + y[:, LANES - 1:LANES]


def _cumsum(logf_t):
    b, rows, n = logf_t.shape
    spec = pl.BlockSpec((1, rows, n), lambda i: (i, 0, 0))
    return pl.pallas_call(
        _cumsum_kernel, grid=(b,), in_specs=[spec], out_specs=spec,
        out_shape=jax.ShapeDtypeStruct(logf_t.shape, F32),
        compiler_params=_params("arbitrary"),
        name="forget_cumsum",
    )(logf_t)


def _ret_kernel(q_ref, k_ref, v_ref, g_ref, s0_ref, ng_ref, nb_ref, o_ref, sn_ref, w_scr, s_scr,
                *, chunk):
    bt, tl, _ = q_ref.shape
    m = bt * tl
    t = pl.program_id(1)

    @pl.when((pl.program_id(0) == 0) & (t == 0))
    def _():
        i = lax.broadcasted_iota(jnp.int32, (m, m), 0)
        j = lax.broadcasted_iota(jnp.int32, (m, m), 1)
        visible = ((i >> _log2(tl)) == (j >> _log2(tl))) & ((j >> _log2(chunk)) <= (i >> _log2(chunk)))
        dist = jnp.abs(i - j).astype(F32)
        for h in range(N_HEADS):
            w_scr[h] = jnp.where(visible, jnp.exp(LOG_G[h] * dist), 0.0)

    @pl.when(t == 0)
    def _():
        s_scr[...] = s0_ref[...]

    head = lax.broadcasted_iota(jnp.int32, (1, GROUP_W), 1) >> _log2(HEAD_DIM)
    log_g = jnp.full((1, GROUP_W), LOG_G[N_HEADS - 1], F32)
    for h in range(N_HEADS - 1):
        log_g = jnp.where(head == h, LOG_G[h], log_g)
    pos = (lax.broadcasted_iota(jnp.int32, (m, 1), 0) & (tl - 1)).astype(F32)
    q_w = jnp.exp((pos + 1.0) * log_g)
    k_w = jnp.exp((tl - 1.0 - pos) * log_g) * HEAD_DIM ** -0.5

    qf = q_ref[...].reshape(m, GROUP_W).astype(F32)
    kf = k_ref[...].reshape(m, GROUP_W).astype(F32)
    v = v_ref[...].reshape(m, GROUP_W)
    gate = g_ref[...].reshape(m, GROUP_W)
    qwb = (qf * q_w).astype(BF16)
    kwb = (kf * k_w).astype(BF16)
    ksb = (kf * HEAD_DIM ** -0.5).astype(BF16)
    lo = _lower_half((m, LANES))
    r = lax.broadcasted_iota(jnp.int32, (LANES, LANES), 0) < HEAD_DIM
    c = lax.broadcasted_iota(jnp.int32, (LANES, LANES), 1) < HEAD_DIM
    same_head = r == c

    for p in range(N_PAIRS):
        sl = slice(p * LANES, (p + 1) * LANES)
        qp, vp = qf[:, sl], v[:, sl]
        q2 = jnp.concatenate([jnp.where(lo, qp, 0.0), jnp.where(lo, 0.0, qp)], axis=0).astype(BF16)
        s = _dot_nt(q2, ksb[:, sl])
        probs = jnp.concatenate([(s[:m] * w_scr[2 * p]).astype(BF16),
                                 (s[m:] * w_scr[2 * p + 1]).astype(BF16)], axis=0)
        o2 = jnp.dot(probs, vp, preferred_element_type=F32)
        intra = jnp.where(lo, o2[:m], o2[m:])
        tile_decay = jnp.where(r, math.exp(tl * LOG_G[2 * p]), math.exp(tl * LOG_G[2 * p + 1]))
        cross = []
        for bi in range(bt):
            rows = slice(bi * tl, (bi + 1) * tl)
            state = s_scr[bi, p]
            cross.append(jnp.dot(qwb[rows, sl], state.astype(BF16), preferred_element_type=F32))
            u = _dot_tn(kwb[rows, sl], vp[rows])
            s_scr[bi, p] = tile_decay * state + jnp.where(same_head, u, 0.0)
        y = intra + (jnp.concatenate(cross, axis=0) if bt > 1 else cross[0])
        yc = y - _half_sums(y, lo) * (1.0 / HEAD_DIM)
        var = _half_sums(yc * yc, lo) * (1.0 / HEAD_DIM)
        yn = yc * lax.rsqrt(var + EPS) * ng_ref[:, sl] + nb_ref[:, sl]
        gp = gate[:, sl]
        out = gp * _sigmoid(gp) * yn
        o_ref[:, :, sl] = out.reshape(bt, tl, LANES).astype(o_ref.dtype)

    @pl.when(t == pl.num_programs(1) - 1)
    def _():
        sn_ref[...] = s_scr[...]


def _retention(rq, rk, rv, rg, s0_pairs, norm_g, norm_b, bt, tl, chunk):
    b, l, _ = rq.shape
    m = bt * tl
    tile = pl.BlockSpec((bt, tl, GROUP_W), lambda i, t: (i, t, 0))
    st = pl.BlockSpec((bt, N_PAIRS, LANES, LANES), lambda i, t: (i, 0, 0, 0))
    return pl.pallas_call(
        functools.partial(_ret_kernel, chunk=chunk),
        grid=(b // bt, l // tl),
        in_specs=[tile, tile, tile, tile, st, _resident((1, GROUP_W)), _resident((1, GROUP_W))],
        out_specs=[tile, st],
        out_shape=[jax.ShapeDtypeStruct((b, l, GROUP_W), BF16),
                   jax.ShapeDtypeStruct(s0_pairs.shape, F32)],
        scratch_shapes=[pltpu.VMEM((N_HEADS, m, m), F32),
                        pltpu.VMEM((bt, N_PAIRS, LANES, LANES), F32)],
        compiler_params=_params("arbitrary", "arbitrary"),
        name="retention",
    )(rq, rk, rv, rg, s0_pairs, norm_g, norm_b)


def _state_to_pairs(s):
    b = s.shape[0]
    s = s.reshape(b, N_PAIRS, 2, HEAD_DIM, HEAD_DIM)
    z = jnp.zeros_like(s[:, :, 0])
    top = jnp.concatenate([s[:, :, 0], z], axis=-1)
    bot = jnp.concatenate([z, s[:, :, 1]], axis=-1)
    return jnp.concatenate([top, bot], axis=-2)


def _pairs_to_state(sp):
    b = sp.shape[0]
    a = sp[:, :, :HEAD_DIM, :HEAD_DIM]
    d = sp[:, :, HEAD_DIM:, HEAD_DIM:]
    return jnp.stack([a, d], axis=2).reshape(b, N_HEADS, HEAD_DIM, HEAD_DIM)


def _softmax_step(s, m, l):
    m_new = jnp.maximum(m, jnp.max(s, axis=-1, keepdims=True))
    alpha = jnp.exp2(m - m_new)
    p = jnp.exp2(s - m_new)
    return p, m_new, alpha * l + jnp.sum(p, axis=-1, keepdims=True), alpha


def _stack_masked(qp, n_streams):
    width = LANES // n_streams
    owner = lax.broadcasted_iota(jnp.int32, qp.shape, 1) >> _log2(width)
    return jnp.concatenate([jnp.where(owner == i, qp, 0.0) for i in range(n_streams)],
                           axis=0).astype(BF16)


def _pair_block(qp, kp, vp, biases, mask, state, frames_last=False):
    n, tq = len(state), qp.shape[0]
    q_stack = _stack_masked(qp, n)
    s = jnp.dot(q_stack, kp, preferred_element_type=F32) if frames_last else _dot_nt(q_stack, kp)
    probs, new = [], []
    for i in range(n):
        si = s[i * tq:(i + 1) * tq]
        if biases is not None:
            si = si + biases[i]
        if mask is not None:
            si = jnp.where(mask, si, MASKED)
        p, m_new, l_new, alpha = _softmax_step(si, state[i][0], state[i][1])
        probs.append(p.astype(BF16))
        new.append((m_new, l_new, alpha))
    probs = jnp.concatenate(probs, axis=0)
    pv = _dot_nt(probs, vp) if frames_last else jnp.dot(probs, vp, preferred_element_type=F32)
    return [(m_new, l_new, alpha * state[i][2] + pv[i * tq:(i + 1) * tq])
            for i, (m_new, l_new, alpha) in enumerate(new)]


def _cached_pair(ref, p):
    pair = ref[0, 0, 2 * p:2 * p + 2]
    return pair.reshape(LANES, pair.shape[-1]).astype(BF16)


def _fresh(n, tq):
    return [(jnp.full((tq, 1), MASKED, F32), jnp.zeros((tq, 1), F32), jnp.zeros((tq, LANES), F32))
            for _ in range(n)]


def _fox_out(state, lo):
    (_, la, acca), (_, lb, accb) = state
    return jnp.where(lo, acca / la, accb / lb)


def _lam(lam_ref, lam_init):
    a = lam_ref[...]
    t1 = jnp.sum(a[0:1] * a[1:2], axis=-1, keepdims=True)
    t2 = jnp.sum(a[2:3] * a[3:4], axis=-1, keepdims=True)
    return jnp.exp(t1) - jnp.exp(t2) + lam_init


def _diff_norm(o, lo, gain, lam_init):
    ms = _half_sums(o * o, lo) * (1.0 / HEAD_DIM)
    return (o * lax.rsqrt(ms + EPS) * gain) * (1.0 - lam_init)


def _diff_out(state, lo, lam, gain, lam_init):
    (_, l0, a0), (_, l1, a1), (_, l2, a2), (_, l3, a3) = state
    o = jnp.where(lo, a0 / l0, a2 / l2) - lam * jnp.where(lo, a1 / l1, a3 / l3)
    return _diff_norm(o, lo, gain, lam_init)


def _rows(r0, n):
    return slice(r0, r0 + n)


def _tile_update(scr, q_scr, k, v, tq, key_bias_fn, row_bias_scr, visibility):
    s_scr, p_scr, m_scr, c_scr, a_scr, acc_scr = scr
    n_rows, tk = s_scr.shape
    n_chunks = tk // LANES
    chunk = lambda j: slice(j * LANES, (j + 1) * LANES)
    for r0 in range(0, n_rows, GROUP_ROWS):
        rows = _rows(r0, GROUP_ROWS)
        s_scr[rows, :] = _dot_nt(q_scr[rows, :], k)
    for r0 in range(0, n_rows, STRIP):
        rows = _rows(r0, STRIP)
        stream, row_in_stream = divmod(r0, tq)
        live = []
        for j in range(n_chunks):
            vis = visibility(row_in_stream, j)
            if vis is None:
                continue
            t = s_scr[rows, chunk(j)]
            if key_bias_fn is not None:
                t = t - key_bias_fn(stream, j)
            if vis is not True:
                t = jnp.where(vis, t, MASKED)
            if key_bias_fn is not None or vis is not True:
                s_scr[rows, chunk(j)] = t
            live.append(t)
        top = functools.reduce(jnp.maximum, live)
        top = jnp.broadcast_to(jnp.max(top, axis=-1, keepdims=True), (STRIP, LANES))
        m_old = m_scr[rows, :]
        if row_bias_scr is not None:
            row_bias = row_bias_scr[rows, :]
            m_new = jnp.maximum(m_old, top + row_bias)
            c_scr[rows, :] = m_new - row_bias
        else:
            m_new = jnp.maximum(m_old, top)
            c_scr[rows, :] = m_new
        a_scr[rows, :] = jnp.exp2(m_old - m_new)
        m_scr[rows, :] = m_new
    for r0 in range(0, n_rows, STRIP):
        rows = _rows(r0, STRIP)
        centre = c_scr[rows, :]
        for j in range(n_chunks):
            if visibility(r0 % tq, j) is None:
                p_scr[rows, chunk(j)] = jnp.zeros((STRIP, LANES), BF16)
            else:
                p_scr[rows, chunk(j)] = jnp.exp2((s_scr[rows, chunk(j)] - centre).astype(BF16))
    ones_col = (lax.broadcasted_iota(jnp.int32, v.shape, 1) == 0).astype(BF16)
    v_ext = jnp.concatenate([v, ones_col], axis=1)
    for r0 in range(0, n_rows, GROUP_ROWS):
        rows = _rows(r0, GROUP_ROWS)
        alpha = a_scr[rows, :]
        acc_scr[rows, :] = (jnp.concatenate([alpha, alpha], axis=1) * acc_scr[rows, :]
                            + jnp.dot(p_scr[rows, :], v_ext, preferred_element_type=F32))


def _all_visible(row0, j):
    return True


def _causal_visibility(row0, j):
    if j * LANES > row0 + STRIP - 1:
        return None
    if j * LANES + LANES - 1 <= row0:
        return True
    rel = (lax.broadcasted_iota(jnp.int32, (STRIP, LANES), 1)
           - lax.broadcasted_iota(jnp.int32, (STRIP, LANES), 0))
    return rel <= row0 - j * LANES


def _chunk_visibility(row0, j):
    q_chunk = row0 // CHUNK
    first, last = j * (LANES // CHUNK), (j + 1) * (LANES // CHUNK) - 1
    if first > q_chunk:
        return None
    if last <= q_chunk:
        return True
    return (lax.broadcasted_iota(jnp.int32, (STRIP, LANES), 1) >> _log2(CHUNK)) <= q_chunk - first


def _init_att_state(m_scr, acc_scr):
    m_scr[...] = jnp.full(m_scr.shape, MASKED, F32)
    acc_scr[...] = jnp.zeros(acc_scr.shape, F32)


def _normalised(acc_scr):
    acc = acc_scr[...]
    return acc[:, :LANES] / acc[:, LANES:LANES + 1]


def _fox_kernel(q_ref, k_ref, v_ref, fq_ref, fk_ref, o_ref,
                q_scr, fq_scr, m_scr, c_scr, a_scr, acc_scr, s_scr, p_scr):
    tq = q_ref.shape[2]
    tk = s_scr.shape[1]
    qi = pl.program_id(1)
    scr = (s_scr, p_scr, m_scr, c_scr, a_scr, acc_scr)
    chunks = tk // LANES
    head_col = lax.broadcasted_iota(jnp.int32, fq_ref.shape[1:], 1)
    lo = _lower_half((tq, LANES))

    def pair(p, carry):
        q_scr[...] = _stack_masked(q_ref[0, p].astype(F32), 2)
        for half in range(2):
            col = jnp.sum(jnp.where(head_col == 2 * p + half, fq_ref[0], 0.0), axis=-1, keepdims=True)
            fq_scr[half * tq:(half + 1) * tq, :] = jnp.broadcast_to(col, (tq, LANES))
        _init_att_state(m_scr, acc_scr)

        def tile(ki, visibility):
            keys = pl.ds(pl.multiple_of(ki * tk, tk), tk)
            key_bias = lambda stream, j: fk_ref[0, 2 * p + stream, ki * chunks + j]
            _tile_update(scr, q_scr, k_ref[0, p, keys, :], v_ref[0, p, keys, :], tq,
                         key_bias, fq_scr, visibility)

        def full_tile(ki, c):
            tile(ki, _all_visible)
            return c

        lax.fori_loop(0, qi, full_tile, 0)
        tile(qi, _causal_visibility)
        o = _normalised(acc_scr)
        o_ref[0, p] = jnp.where(lo, o[:tq], o[tq:]).astype(o_ref.dtype)
        return carry

    lax.fori_loop(0, N_PAIRS, pair, 0)


def _diff_kernel(q_ref, k_ref, v_ref, lam_ref, g_ref, o_ref,
                 q_scr, m_scr, c_scr, a_scr, acc_scr, s_scr, p_scr, *, lam_init):
    tq = q_ref.shape[2]
    tk = s_scr.shape[1]
    qi = pl.program_id(1)
    scr = (s_scr, p_scr, m_scr, c_scr, a_scr, acc_scr)
    lo = _lower_half((tq, LANES))
    lam = _lam(lam_ref, lam_init)

    def pair(p, carry):
        q_scr[...] = _stack_masked(q_ref[0, p].astype(F32), 4)
        _init_att_state(m_scr, acc_scr)

        def tile(ki, visibility):
            keys = pl.ds(pl.multiple_of(ki * tk, tk), tk)
            _tile_update(scr, q_scr, k_ref[0, p, keys, :], v_ref[0, p, keys, :], tq,
                         None, None, visibility)

        def full_tile(ki, c):
            tile(ki, _all_visible)
            return c

        lax.fori_loop(0, qi, full_tile, 0)
        tile(qi, _chunk_visibility)
        o = _normalised(acc_scr)
        o = (jnp.where(lo, o[:tq], o[2 * tq:3 * tq]) - lam * jnp.where(lo, o[tq:2 * tq], o[3 * tq:]))
        o_ref[0, p] = _diff_norm(o, lo, g_ref[p], lam_init).astype(o_ref.dtype)
        return carry

    lax.fori_loop(0, N_PAIRS, pair, 0)


def _prompt_att_call(kernel_fn, name, n_streams, tile, q, k, v, extra, extra_specs, row_bias):
    b, _, l, _ = q.shape
    tq = tk = tile
    assert l % tq == 0 and tq % CHUNK == 0
    rows = n_streams * tq
    qspec = pl.BlockSpec((1, N_PAIRS, tq, LANES), lambda i, t: (i, 0, t, 0))
    kspec = pl.BlockSpec((1, N_PAIRS, l, LANES), lambda i, t: (i, 0, 0, 0), pipeline_mode=pl.Buffered(1))
    stats = pltpu.VMEM((rows, LANES), F32)
    return pl.pallas_call(
        kernel_fn, grid=(b, l // tq),
        in_specs=[qspec, kspec, kspec] + extra_specs,
        out_specs=qspec,
        out_shape=jax.ShapeDtypeStruct(q.shape, BF16),
        scratch_shapes=[pltpu.VMEM((rows, LANES), BF16)] + ([stats] if row_bias else [])
        + [stats, stats, stats, pltpu.VMEM((rows, 2 * LANES), F32),
           pltpu.VMEM((rows, tk), F32), pltpu.VMEM((rows, tk), BF16)],
        compiler_params=_params("arbitrary", "arbitrary"),
        name=name,
    )(q, k, v, *extra)


def _fox_prompt(q, k, v, fq, fk_rep):
    tq = min(FOX_TILE, q.shape[2])
    specs = [pl.BlockSpec((1, tq, fq.shape[-1]), lambda i, t: (i, t, 0)),
             pl.BlockSpec((1,) + fk_rep.shape[1:], lambda i, t: (i, 0, 0, 0, 0),
                          pipeline_mode=pl.Buffered(1))]
    return _prompt_att_call(_fox_kernel, "fox_attention", 2, tq, q, k, v, (fq, fk_rep), specs, True)


def _diff_prompt(q, k, v, lam_rows, gain, lam_init):
    gain_pairs = gain.reshape(N_PAIRS, 1, LANES)
    specs = [_resident(lam_rows.shape), _resident(gain_pairs.shape)]
    return _prompt_att_call(functools.partial(_diff_kernel, lam_init=lam_init), "diff_attention", 4,
                            min(DIFF_TILE, q.shape[2]), q, k, v, (lam_rows, gain_pairs), specs, False)


def _fox_sample_kernel(q_ref, kp_ref, vp_ref, kn_ref, vn_ref, fq_ref, fkp_ref, fkn_ref, o_ref):
    tq, tn = q_ref.shape[2], kn_ref.shape[2]
    fq, fkp, fkn = fq_ref[0], fkp_ref[0], fkn_ref[0]
    causal = lax.broadcasted_iota(jnp.int32, (tq, tn), 1) <= lax.broadcasted_iota(jnp.int32, (tq, tn), 0)
    lo = _lower_half((tq, LANES))
    for p in range(N_PAIRS):
        idx = (2 * p, 2 * p + 1)
        qf = q_ref[0, p].astype(F32)
        state = _pair_block(qf, _cached_pair(kp_ref, p), _cached_pair(vp_ref, p),
                            [fq[:, h:h + 1] - fkp[h:h + 1, :] for h in idx], None, _fresh(2, tq),
                            frames_last=True)
        state = _pair_block(qf, kn_ref[0, p], vn_ref[0, p],
                            [fq[:, h:h + 1] - fkn[h:h + 1, :] for h in idx], causal, state)
        o_ref[0, p] = _fox_out(state, lo).astype(o_ref.dtype)


def _diff_sample_kernel(q_ref, kp_ref, vp_ref, kn_ref, vn_ref, lam_ref, g_ref, o_ref,
                        *, lam_init, past_len, n_new):
    tq, tn = q_ref.shape[2], kn_ref.shape[2]
    col = lax.broadcasted_iota(jnp.int32, (tq, tn), 1)
    row = lax.broadcasted_iota(jnp.int32, (tq, tn), 0)
    shift = _log2(CHUNK)
    visible = (((past_len + col) >> shift) <= ((past_len + row) >> shift)) & (col < n_new)
    lo = _lower_half((tq, LANES))
    lam = _lam(lam_ref, lam_init)
    for p in range(N_PAIRS):
        sl = slice(p * LANES, (p + 1) * LANES)
        qf = q_ref[0, p].astype(F32)
        state = _pair_block(qf, _cached_pair(kp_ref, p), _cached_pair(vp_ref, p), None, None,
                            _fresh(4, tq), frames_last=True)
        state = _pair_block(qf, kn_ref[0, p], vn_ref[0, p], None, visible, state)
        o_ref[0, p] = _diff_out(state, lo, lam, g_ref[:, sl], lam_init).astype(o_ref.dtype)


def _sample_specs(q, k_past, k_new, layer):
    b = q.shape[0]
    pairs = lambda a: pl.BlockSpec((1,) + a.shape[1:], lambda i: (i, 0, 0, 0))
    past = pl.BlockSpec((1, 1) + k_past.shape[2:], lambda i: (layer, i, 0, 0, 0))
    return b, pairs(q), past, pairs(k_new)


def _fox_sample(q, k_past, v_past, k_new, v_new, fq, fk_past, fk_new, layer):
    b, qspec, pspec, nspec = _sample_specs(q, k_past, k_new, layer)
    frow = lambda a: pl.BlockSpec((1,) + a.shape[1:], lambda i: (i, 0, 0))
    return pl.pallas_call(
        _fox_sample_kernel, grid=(b,),
        in_specs=[qspec, pspec, pspec, nspec, nspec, frow(fq), frow(fk_past), frow(fk_new)],
        out_specs=qspec,
        out_shape=jax.ShapeDtypeStruct(q.shape, BF16),
        compiler_params=_params("arbitrary"),
        name="fox_attention_cached",
    )(q, k_past, v_past, k_new, v_new, fq, fk_past, fk_new)


def _diff_sample(q, k_past, v_past, k_new, v_new, lam_rows, gain, lam_init, n_new, layer):
    b, qspec, pspec, nspec = _sample_specs(q, k_past, k_new, layer)
    past_len = k_past.shape[-1]
    assert past_len % CHUNK == 0
    return pl.pallas_call(
        functools.partial(_diff_sample_kernel, lam_init=lam_init, past_len=past_len, n_new=n_new),
        grid=(b,),
        in_specs=[qspec, pspec, pspec, nspec, nspec, _resident(lam_rows.shape), _resident((1, GROUP_W))],
        out_specs=qspec,
        out_shape=jax.ShapeDtypeStruct(q.shape, BF16),
        compiler_params=_params("arbitrary"),
        name="diff_attention_cached",
    )(q, k_past, v_past, k_new, v_new, lam_rows, gain)


def _mlp_kernel(x_ref, r_ref, f_ref, d_ref, g1_ref, sc_ref, sh_ref, g2_ref, ng_ref,
                wo_ref, wu_ref, wd_ref, fg_ref, o_ref, *, final_norm):
    bt, tl, d = x_ref.shape
    m = bt * tl
    mix = jnp.dot(r_ref[...].reshape(m, GROUP_W), wo_ref[0], preferred_element_type=F32)
    for i, ref in ((1, f_ref), (2, d_ref)):
        for p in range(N_PAIRS):
            mix = mix + jnp.dot(ref[:, p].reshape(m, LANES), wo_ref[i, p * LANES:(p + 1) * LANES, :],
                                preferred_element_type=F32)
    x = x_ref[...] + g1_ref[...] * mix.reshape(bt, tl, d)
    hb = _rms_mod(x, ng_ref[...], sc_ref[...], sh_ref[...]).reshape(m, d).astype(BF16)
    d_ff = wu_ref.shape[1]
    step = d_ff // 4
    y = jnp.zeros((m, d), F32)
    for j in range(0, d_ff, step):
        u = jnp.maximum(jnp.dot(hb, wu_ref[:, j:j + step], preferred_element_type=F32), 0.0)
        y = y + jnp.dot((u * u).astype(BF16), wd_ref[j:j + step, :], preferred_element_type=F32)
    x = x + g2_ref[...] * y.reshape(bt, tl, d)
    if final_norm:
        x = x * lax.rsqrt(jnp.mean(x * x, axis=-1, keepdims=True) + EPS) * fg_ref[...]
    o_ref[...] = x


def _mlp(x, r, f, dd, g1, sc2, sh2, g2, norm_g, w_out3, w_up, w_down, final_g, bt, tl, final_norm):
    b, l, d = x.shape
    tile = lambda w: pl.BlockSpec((bt, tl, w), lambda i, t: (i, t, 0))
    pairs = pl.BlockSpec((bt, N_PAIRS, tl, LANES), lambda i, t: (i, 0, t, 0))
    mod = pl.BlockSpec((bt, 1, d), lambda i, t: (i, 0, 0))
    return pl.pallas_call(
        functools.partial(_mlp_kernel, final_norm=final_norm),
        grid=(b // bt, l // tl),
        in_specs=[tile(d), tile(GROUP_W), pairs, pairs, mod, mod, mod, mod,
                  _resident((1, d)), _resident(w_out3.shape), _resident(w_up.shape),
                  _resident(w_down.shape), _resident((1, d))],
        out_specs=tile(d),
        out_shape=jax.ShapeDtypeStruct(x.shape, F32),
        compiler_params=_params("arbitrary", "arbitrary"),
        name="out_proj_mlp",
    )(x, r, f, dd, g1, sc2, sh2, g2, norm_g, w_out3, w_up, w_down, final_g)


def _rope_tables(pos, half, batch_reps):
    inv = ROPE_THETA ** (-jnp.arange(half, dtype=F32) / half)
    ang = pos.astype(F32)[:, None] * inv[None, :]
    cos, sin = jnp.cos(ang), jnp.sin(ang)
    reps = GROUP_W // (2 * half)
    cos = jnp.tile(jnp.concatenate([cos, cos], axis=-1), (batch_reps, reps))
    sin = jnp.tile(jnp.concatenate([-sin, sin], axis=-1), (batch_reps, reps))
    return cos, sin


def _pad_axis(a, axis, size):
    pad = [(0, 0)] * a.ndim
    pad[axis] = (0, size - a.shape[axis])
    return jnp.pad(a, pad)


def _forget_cumsum(logf_all):
    lk = logf_all.shape[1]
    x = jnp.transpose(logf_all, (0, 2, 1))
    x = _pad_axis(_pad_axis(x, 1, F_ROWS), 2, -(-lk // LANES) * LANES)
    return _cumsum(x)


def _group_layer(x, mods, tabs, past, lw, lam_init, bt, tl, final_g, final_norm, layer, depth,
                 cache_rows):
    b, l, d = x.shape
    sh1, sc1, g1, sh2, sc2, g2 = mods
    (rq, rk, rv, rg, fq, fk_f, fk_b, fv_f, fv_b, logf_pad,
     dq, dk_f, dk_b, dv_f, dv_b) = _inproj(x, sc1, sh1, lw["norm1_g"], lw["w_main"], lw["w_ff"],
                                           lw["b_ff"], tabs, bt, tl, layer, depth, cache_rows)
    logf = logf_pad[:, :, :N_HEADS]
    if past is None:
        s0 = jnp.zeros((b, N_HEADS, HEAD_DIM, HEAD_DIM), F32)
        f_t = _forget_cumsum(logf)
        fq_cum = jnp.transpose(f_t[:, :N_HEADS + 2, :l], (0, 2, 1))
        fk_rep = jnp.broadcast_to(f_t[:, :N_HEADS].reshape(b, N_HEADS, l // LANES, 1, LANES),
                                  (b, N_HEADS, l // LANES, STRIP, LANES))
        f_out = _fox_prompt(fq, fk_b, fv_b, fq_cum, fk_rep)
        d_out = _diff_prompt(dq, dk_b, dv_b, lw["lam_rows"], lw["diff_norm_g"], lam_init)
    else:
        pfk, pfv, plogf, pdk, pdv, s0 = past
        past_len = pfk.shape[-1]
        f_t = _forget_cumsum(jnp.concatenate([plogf, logf], axis=1))
        fq_cum = jnp.transpose(f_t[:, :N_HEADS + 2, past_len:past_len + l], (0, 2, 1))
        new_rows = lambda a: _pad_axis(a, 2, LANES)
        f_out = _fox_sample(fq, pfk, pfv, new_rows(fk_b), new_rows(fv_b), fq_cum,
                            f_t[:, :, :past_len], f_t[:, :, past_len:past_len + LANES], layer)
        d_out = _diff_sample(dq, pdk, pdv, new_rows(dk_b), new_rows(dv_b), lw["lam_rows"],
                             lw["diff_norm_g"], lam_init, l, layer)
    r_out, s_pairs = _retention(rq, rk, rv, rg, _state_to_pairs(s0), lw["ret_norm_g"],
                                lw["ret_norm_b"], bt, tl, min(CHUNK, l))
    x = _mlp(x, r_out, f_out, d_out, g1, sc2, sh2, g2, lw["norm2_g"], lw["w_out3"], lw["w_up"],
             lw["w_down"], final_g, bt, tl, final_norm)
    return x, (fk_f, fv_f, dk_f, dv_f), logf, _pairs_to_state(s_pairs)


def kernel(x_prompt, x_sample, cache_fox_k, cache_fox_v, cache_fox_logf, cache_diff_k, cache_diff_v,
           state_ret, c_prompt, c_sample, norm1_g, norm2_g, w_ada, b_ada, w_in, b_forget,
           ret_norm_g, ret_norm_b, lam_q1, lam_k1, lam_q2, lam_k2, diff_norm_g, w_out, w_up, w_down,
           final_g):
    depth, d = norm1_g.shape
    bp, lp, _ = x_prompt.shape
    bs, ls, _ = x_sample.shape
    past_len = cache_fox_k.shape[2]

    rows = -(-(bp + bs) // 8) * 8
    c_all = _pad_axis(jnp.concatenate([c_prompt, c_sample], axis=0), 0, rows)
    mod = _ada(c_all, w_ada, b_ada)

    def mods(li, r0, n):
        return [mod[li, r0:r0 + n, i * d:(i + 1) * d].reshape(n, 1, d) for i in range(6)]

    ff0 = 7 * GROUP_W
    w_main = jnp.concatenate([w_in[:, :, :ff0], w_in[:, :, ff0 + N_HEADS:]], axis=-1).astype(BF16)
    w_ff = _pad_axis(w_in[:, :, ff0:ff0 + N_HEADS], 2, LANES).astype(BF16)
    b_ff = _pad_axis(b_forget, 1, LANES).reshape(depth, 1, LANES)
    w_out3 = w_out.reshape(depth, 3, GROUP_W, d).astype(BF16)
    w_up_b, w_down_b = w_up.astype(BF16), w_down.astype(BF16)
    lam_rows = _pad_axis(_pad_axis(jnp.stack([lam_q1, lam_k1, lam_q2, lam_k2], axis=1), 2, LANES), 1, 8)
    final_row = final_g.reshape(1, d)

    tl_p = min(TOK_TILE, lp)
    p_pos = jnp.arange(lp, dtype=jnp.int32)
    s_pos = past_len + jnp.arange(ls, dtype=jnp.int32)
    tabs_p = _rope_tables(p_pos, HEAD_DIM // 2, 1) + _rope_tables(p_pos, DIFF_QK // 2, 1)
    tabs_s = _rope_tables(s_pos, HEAD_DIM // 2, bs) + _rope_tables(s_pos, DIFF_QK // 2, bs)

    frames_last = lambda a: jnp.transpose(a, (0, 1, 3, 4, 2))
    fox_k_t, fox_v_t = frames_last(cache_fox_k), frames_last(cache_fox_v)
    diff_k_t, diff_v_t = frames_last(cache_diff_k), frames_last(cache_diff_v)

    xp, xs = x_prompt, x_sample
    p_rows, s_rows = None, None
    p_logf, p_state, s_logf, s_state = [], [], [], []
    for li in range(depth):
        lam_init = 0.8 - 0.6 * math.exp(-0.3 * li)
        lw = dict(norm1_g=norm1_g[li].reshape(1, d), norm2_g=norm2_g[li].reshape(1, d),
                  w_main=w_main[li], w_ff=w_ff[li], b_ff=b_ff[li],
                  ret_norm_g=ret_norm_g[li].reshape(1, GROUP_W), ret_norm_b=ret_norm_b[li].reshape(1, GROUP_W),
                  lam_rows=lam_rows[li], diff_norm_g=diff_norm_g[li].reshape(1, GROUP_W),
                  w_out3=w_out3[li], w_up=w_up_b[li], w_down=w_down_b[li])
        last = li == depth - 1
        xp, p_rows, logf, state = _group_layer(xp, mods(li, 0, bp), tabs_p, None, lw, lam_init, 1, tl_p,
                                               final_row, last, li, depth, p_rows)
        p_logf.append(logf)
        p_state.append(state)
        past = (fox_k_t, fox_v_t, cache_fox_logf[li], diff_k_t, diff_v_t, state_ret[li])
        xs, s_rows, logf, state = _group_layer(xs, mods(li, bp, bs), tabs_s, past, lw, lam_init, bs, ls,
                                               final_row, last, li, depth, s_rows)
        s_logf.append(logf)
        s_state.append(state)

    def group_outputs(rows, logf, state):
        fk, fv, dk, dv = (a.reshape(a.shape[:3] + (N_HEADS, HEAD_DIM)) for a in rows)
        return fk, fv, jnp.stack(logf), dk, dv, jnp.stack(state)

    return (xp, xs) + group_outputs(p_rows, p_logf, p_state) + group_outputs(s_rows, s_logf, s_state)
```

```python
import functools
import math

import jax
import jax.numpy as jnp
from jax import lax
from jax.experimental import pallas as pl
from jax.experimental.pallas import tpu as pltpu

F32 = jnp.float32
BF16 = jnp.bfloat16

CHUNK = 64
HEAD_DIM = 64
N_HEADS = 6
GROUP_W = N_HEADS * HEAD_DIM
LANES = 128
N_PAIRS = GROUP_W // LANES
DIFF_QK = HEAD_DIM // 2
ROPE_THETA = 10000.0
EPS = 1e-6
MASKED = -1e30
LOG2E = math.log2(math.e)
STRIP = 16
GROUP_ROWS = 128
LOG_G = tuple(math.log1p(-(2.0 ** (-5.0 - h))) for h in range(N_HEADS))
TOK_TILE = 512
FOX_TILE = 1024
DIFF_TILE = 1024
F_ROWS = 16
VMEM_LIMIT = 56 * 1024 * 1024


def _params(*sem):
    return pltpu.CompilerParams(dimension_semantics=sem, vmem_limit_bytes=VMEM_LIMIT)


def _resident(shape):
    zeros = (0,) * len(shape)
    return pl.BlockSpec(shape, lambda *_: zeros, pipeline_mode=pl.Buffered(1))


def _sigmoid(x):
    return 1.0 / (1.0 + jnp.exp(-x))


def _log2(n):
    assert n > 0 and n & (n - 1) == 0, n
    return n.bit_length() - 1


def _dot_nt(a, b):
    return lax.dot_general(a, b, (((1,), (1,)), ((), ())), preferred_element_type=F32)


def _dot_tn(a, b):
    return lax.dot_general(a, b, (((0,), (0,)), ((), ())), preferred_element_type=F32)


def _rms_mod(x, g, scale, shift):
    y = x * lax.rsqrt(jnp.mean(x * x, axis=-1, keepdims=True) + EPS)
    return (y * g) * (1.0 + scale) + shift


def _lower_half(shape):
    return lax.broadcasted_iota(jnp.int32, shape, len(shape) - 1) < HEAD_DIM


def _half_sums(x, lo):
    a = jnp.sum(jnp.where(lo, x, 0.0), axis=-1, keepdims=True)
    b = jnp.sum(jnp.where(lo, 0.0, x), axis=-1, keepdims=True)
    return jnp.where(lo, a, b)


def _ada_kernel(c_ref, w_ref, b_ref, o_ref):
    c = c_ref[...]
    a = (c * _sigmoid(c)).astype(BF16)
    o_ref[0] = jnp.dot(a, w_ref[0].astype(BF16), preferred_element_type=F32) + b_ref[0]


def _ada(c_all, w_ada, b_ada):
    depth, d, n = w_ada.shape
    rows = c_all.shape[0]
    tn = n // 4
    return pl.pallas_call(
        _ada_kernel,
        grid=(depth, n // tn),
        in_specs=[
            pl.BlockSpec((rows, d), lambda l, j: (0, 0)),
            pl.BlockSpec((1, d, tn), lambda l, j: (l, 0, j)),
            pl.BlockSpec((1, 1, tn), lambda l, j: (l, 0, j)),
        ],
        out_specs=pl.BlockSpec((1, rows, tn), lambda l, j: (l, 0, j)),
        out_shape=jax.ShapeDtypeStruct((depth, rows, n), F32),
        compiler_params=_params("arbitrary", "arbitrary"),
        name="ada_mod",
    )(c_all, w_ada, b_ada.reshape(depth, 1, n))


def _rope(y, cos, sin_signed, half):
    width = y.shape[-1]
    lane = lax.broadcasted_iota(jnp.int32, y.shape, 1)
    ahead = pltpu.roll(y, width - half, axis=1)
    behind = pltpu.roll(y, half, axis=1)
    swapped = jnp.where((lane & half) == 0, ahead, behind)
    return y * cos + swapped * sin_signed


N_INPROJ_IN = 11


def _inproj_kernel(*refs):
    (x_ref, sc_ref, sh_ref, g_ref, wm_ref, wf_ref, bf_ref,
     c64_ref, s64_ref, c32_ref, s32_ref) = refs[:N_INPROJ_IN]
    (rq_o, rk_o, rv_o, rg_o, fq_o, fkf_o, fkb_o, fvf_o, fvb_o, lf_o,
     dq_o, dkf_o, dkb_o, dvf_o, dvb_o) = refs[-15:]
    bt, tl, d = x_ref.shape
    m = bt * tl
    h = _rms_mod(x_ref[...], g_ref[...], sc_ref[...], sh_ref[...])
    hb = h.reshape(m, d).astype(BF16)

    def piece(i):
        return jnp.dot(hb, wm_ref[:, i * GROUP_W:(i + 1) * GROUP_W], preferred_element_type=F32)

    def put(y, *refs):
        for ref in refs:
            if len(ref.shape) == 4 and ref.shape[-1] == LANES:
                for p in range(N_PAIRS):
                    yp = y[:, p * LANES:(p + 1) * LANES]
                    ref[:, p] = yp.reshape(bt, tl, LANES).astype(ref.dtype)
            else:
                ref[...] = y.reshape(ref.shape).astype(ref.dtype)

    c64, s64 = c64_ref[...], s64_ref[...]
    c32, s32 = c32_ref[...], s32_ref[...]
    put(_rope(piece(0), c64, s64, HEAD_DIM // 2), rq_o)
    put(_rope(piece(1), c64, s64, HEAD_DIM // 2), rk_o)
    put(piece(2), rv_o)
    put(piece(3), rg_o)
    put(piece(4) * (HEAD_DIM ** -0.5 * LOG2E), fq_o)
    put(piece(5), fkf_o, fkb_o)
    put(piece(6), fvf_o, fvb_o)
    z = jnp.dot(hb, wf_ref[...], preferred_element_type=F32) + bf_ref[...]
    put(jnp.minimum(z, 0.0) - jnp.log1p(jnp.exp(-jnp.abs(z))), lf_o)
    put(_rope(piece(7), c32, s32, DIFF_QK // 2) * (DIFF_QK ** -0.5 * LOG2E), dq_o)
    put(_rope(piece(8), c32, s32, DIFF_QK // 2), dkf_o, dkb_o)
    put(piece(9), dvf_o, dvb_o)


CACHE_OUTS = (5, 7, 11, 13)


def _inproj(x, scale, shift, gain, w_main, w_ff, b_ff, tabs, bt, tl, layer, depth, cache_rows):
    b, l, d = x.shape
    m = bt * tl
    grid = (b // bt, l // tl)
    tile = lambda w: pl.BlockSpec((bt, tl, w), lambda i, t: (i, t, 0))
    mod = pl.BlockSpec((bt, 1, d), lambda i, t: (i, 0, 0))
    tab = pl.BlockSpec((m, GROUP_W), lambda i, t: (t, 0))
    pairs = (pl.BlockSpec((bt, N_PAIRS, tl, LANES), lambda i, t: (i, 0, t, 0)),
             jax.ShapeDtypeStruct((b, N_PAIRS, l, LANES), BF16))
    flat = lambda w, dt: (tile(w), jax.ShapeDtypeStruct((b, l, w), dt))
    stacked = (pl.BlockSpec((1, bt, tl, GROUP_W), lambda i, t: (layer, i, t, 0)),
               jax.ShapeDtypeStruct((depth, b, l, GROUP_W), F32))
    outs = [
        flat(GROUP_W, BF16), flat(GROUP_W, BF16), flat(GROUP_W, BF16), flat(GROUP_W, F32),
        pairs, stacked, pairs, stacked, pairs,
        flat(LANES, F32),
        pairs, stacked, pairs, stacked, pairs,
    ]
    in_specs = [tile(d), mod, mod, _resident((1, d)), _resident(w_main.shape),
                _resident(w_ff.shape), _resident((1, LANES)), tab, tab, tab, tab]
    assert len(in_specs) == N_INPROJ_IN
    aliases = {}
    if cache_rows is not None:
        in_specs += [pl.BlockSpec(memory_space=pl.ANY)] * len(CACHE_OUTS)
        aliases = {N_INPROJ_IN + n: out for n, out in enumerate(CACHE_OUTS)}
    return pl.pallas_call(
        _inproj_kernel,
        grid=grid,
        in_specs=in_specs,
        out_specs=[spec for spec, _ in outs],
        out_shape=[shape for _, shape in outs],
        input_output_aliases=aliases,
        compiler_params=_params("arbitrary", "arbitrary"),
        name="in_proj",
    )(x, scale, shift, gain, w_main, w_ff, b_ff, *tabs, *(cache_rows or ()))


def _cumsum_kernel(x_ref, o_ref):
    rows, n = x_ref.shape[1], x_ref.shape[2]
    r = lax.broadcasted_iota(jnp.int32, (LANES, LANES), 0)
    c = lax.broadcasted_iota(jnp.int32, (LANES, LANES), 1)
    upper = (r <= c).astype(BF16)
    carry = jnp.zeros((rows, 1), F32)
    for j in range(n // LANES):
        x = x_ref[0, :, j * LANES:(j + 1) * LANES]
        hi = x.astype(BF16)
        r1 = x - hi.astype(F32)
        mid = r1.astype(BF16)
        lo = (r1 - mid.astype(F32)).astype(BF16)
        y = jnp.dot(jnp.concatenate([hi, mid, lo], axis=0), upper, preferred_element_type=F32)
        y = y[:rows] + y[rows:2 * rows] + y[2 * rows:]
        o_ref[0, :, j * LANES:(j + 1) * LANES] = (y + carry) * LOG2E
        carry = carry + y[:, LANES - 1:LANES]


def _cumsum(logf_t):
    b, rows, n = logf_t.shape
    spec = pl.BlockSpec((1, rows, n), lambda i: (i, 0, 0))
    return pl.pallas_call(
        _cumsum_kernel, grid=(b,), in_specs=[spec], out_specs=spec,
        out_shape=jax.ShapeDtypeStruct(logf_t.shape, F32),
        compiler_params=_params("arbitrary"),
        name="forget_cumsum",
    )(logf_t)


def _ret_kernel(q_ref, k_ref, v_ref, g_ref, s0_ref, ng_ref, nb_ref, o_ref, sn_ref, w_scr, s_scr,
                *, chunk):
    bt, tl, _ = q_ref.shape
    m = bt * tl
    t = pl.program_id(1)

    @pl.when((pl.program_id(0) == 0) & (t == 0))
    def _():
        i = lax.broadcasted_iota(jnp.int32, (m, m), 0)
        j = lax.broadcasted_iota(jnp.int32, (m, m), 1)
        visible = ((i >> _log2(tl)) == (j >> _log2(tl))) & ((j >> _log2(chunk)) <= (i >> _log2(chunk)))
        dist = jnp.abs(i - j).astype(F32)
        for h in range(N_HEADS):
            w_scr[h] = jnp.where(visible, jnp.exp(LOG_G[h] * dist), 0.0)

    @pl.when(t == 0)
    def _():
        s_scr[...] = s0_ref[...]

    head = lax.broadcasted_iota(jnp.int32, (1, GROUP_W), 1) >> _log2(HEAD_DIM)
    log_g = jnp.full((1, GROUP_W), LOG_G[N_HEADS - 1], F32)
    for h in range(N_HEADS - 1):
        log_g = jnp.where(head == h, LOG_G[h], log_g)
    pos = (lax.broadcasted_iota(jnp.int32, (m, 1), 0) & (tl - 1)).astype(F32)
    q_w = jnp.exp((pos + 1.0) * log_g)
    k_w = jnp.exp((tl - 1.0 - pos) * log_g) * HEAD_DIM ** -0.5

    qf = q_ref[...].reshape(m, GROUP_W).astype(F32)
    kf = k_ref[...].reshape(m, GROUP_W).astype(F32)
    v = v_ref[...].reshape(m, GROUP_W)
    gate = g_ref[...].reshape(m, GROUP_W)
    qwb = (qf * q_w).astype(BF16)
    kwb = (kf * k_w).astype(BF16)
    ksb = (kf * HEAD_DIM ** -0.5).astype(BF16)
    lo = _lower_half((m, LANES))
    r = lax.broadcasted_iota(jnp.int32, (LANES, LANES), 0) < HEAD_DIM
    c = lax.broadcasted_iota(jnp.int32, (LANES, LANES), 1) < HEAD_DIM
    same_head = r == c

    for p in range(N_PAIRS):
        sl = slice(p * LANES, (p + 1) * LANES)
        qp, vp = qf[:, sl], v[:, sl]
        q2 = jnp.concatenate([jnp.where(lo, qp, 0.0), jnp.where(lo, 0.0, qp)], axis=0).astype(BF16)
        s = _dot_nt(q2, ksb[:, sl])
        probs = jnp.concatenate([(s[:m] * w_scr[2 * p]).astype(BF16),
                                 (s[m:] * w_scr[2 * p + 1]).astype(BF16)], axis=0)
        o2 = jnp.dot(probs, vp, preferred_element_type=F32)
        intra = jnp.where(lo, o2[:m], o2[m:])
        tile_decay = jnp.where(r, math.exp(tl * LOG_G[2 * p]), math.exp(tl * LOG_G[2 * p + 1]))
        cross = []
        for bi in range(bt):
            rows = slice(bi * tl, (bi + 1) * tl)
            state = s_scr[bi, p]
            cross.append(jnp.dot(qwb[rows, sl], state.astype(BF16), preferred_element_type=F32))
            u = _dot_tn(kwb[rows, sl], vp[rows])
            s_scr[bi, p] = tile_decay * state + jnp.where(same_head, u, 0.0)
        y = intra + (jnp.concatenate(cross, axis=0) if bt > 1 else cross[0])
        yc = y - _half_sums(y, lo) * (1.0 / HEAD_DIM)
        var = _half_sums(yc * yc, lo) * (1.0 / HEAD_DIM)
        yn = yc * lax.rsqrt(var + EPS) * ng_ref[:, sl] + nb_ref[:, sl]
        gp = gate[:, sl]
        out = gp * _sigmoid(gp) * yn
        o_ref[:, :, sl] = out.reshape(bt, tl, LANES).astype(o_ref.dtype)

    @pl.when(t == pl.num_programs(1) - 1)
    def _():
        sn_ref[...] = s_scr[...]


def _retention(rq, rk, rv, rg, s0_pairs, norm_g, norm_b, bt, tl, chunk):
    b, l, _ = rq.shape
    m = bt * tl
    tile = pl.BlockSpec((bt, tl, GROUP_W), lambda i, t: (i, t, 0))
    st = pl.BlockSpec((bt, N_PAIRS, LANES, LANES), lambda i, t: (i, 0, 0, 0))
    return pl.pallas_call(
        functools.partial(_ret_kernel, chunk=chunk),
        grid=(b // bt, l // tl),
        in_specs=[tile, tile, tile, tile, st, _resident((1, GROUP_W)), _resident((1, GROUP_W))],
        out_specs=[tile, st],
        out_shape=[jax.ShapeDtypeStruct((b, l, GROUP_W), BF16),
                   jax.ShapeDtypeStruct(s0_pairs.shape, F32)],
        scratch_shapes=[pltpu.VMEM((N_HEADS, m, m), F32),
                        pltpu.VMEM((bt, N_PAIRS, LANES, LANES), F32)],
        compiler_params=_params("arbitrary", "arbitrary"),
        name="retention",
    )(rq, rk, rv, rg, s0_pairs, norm_g, norm_b)


def _state_to_pairs(s):
    b = s.shape[0]
    s = s.reshape(b, N_PAIRS, 2, HEAD_DIM, HEAD_DIM)
    z = jnp.zeros_like(s[:, :, 0])
    top = jnp.concatenate([s[:, :, 0], z], axis=-1)
    bot = jnp.concatenate([z, s[:, :, 1]], axis=-1)
    return jnp.concatenate([top, bot], axis=-2)


def _pairs_to_state(sp):
    b = sp.shape[0]
    a = sp[:, :, :HEAD_DIM, :HEAD_DIM]
    d = sp[:, :, HEAD_DIM:, HEAD_DIM:]
    return jnp.stack([a, d], axis=2).reshape(b, N_HEADS, HEAD_DIM, HEAD_DIM)


def _softmax_step(s, m, l):
    m_new = jnp.maximum(m, jnp.max(s, axis=-1, keepdims=True))
    alpha = jnp.exp2(m - m_new)
    p = jnp.exp2(s - m_new)
    return p, m_new, alpha * l + jnp.sum(p, axis=-1, keepdims=True), alpha


def _stack_masked(qp, n_streams):
    width = LANES // n_streams
    owner = lax.broadcasted_iota(jnp.int32, qp.shape, 1) >> _log2(width)
    return jnp.concatenate([jnp.where(owner == i, qp, 0.0) for i in range(n_streams)],
                           axis=0).astype(BF16)


def _pair_block(qp, kp, vp, biases, mask, state, frames_last=False):
    n, tq = len(state), qp.shape[0]
    q_stack = _stack_masked(qp, n)
    s = jnp.dot(q_stack, kp, preferred_element_type=F32) if frames_last else _dot_nt(q_stack, kp)
    probs, new = [], []
    for i in range(n):
        si = s[i * tq:(i + 1) * tq]
        if biases is not None:
            si = si + biases[i]
        if mask is not None:
            si = jnp.where(mask, si, MASKED)
        p, m_new, l_new, alpha = _softmax_step(si, state[i][0], state[i][1])
        probs.append(p.astype(BF16))
        new.append((m_new, l_new, alpha))
    probs = jnp.concatenate(probs, axis=0)
    pv = _dot_nt(probs, vp) if frames_last else jnp.dot(probs, vp, preferred_element_type=F32)
    return [(m_new, l_new, alpha * state[i][2] + pv[i * tq:(i + 1) * tq])
            for i, (m_new, l_new, alpha) in enumerate(new)]


def _cached_pair(ref, p):
    pair = ref[0, 0, 2 * p:2 * p + 2]
    return pair.reshape(LANES, pair.shape[-1]).astype(BF16)


def _fresh(n, tq):
    return [(jnp.full((tq, 1), MASKED, F32), jnp.zeros((tq, 1), F32), jnp.zeros((tq, LANES), F32))
            for _ in range(n)]


def _fox_out(state, lo):
    (_, la, acca), (_, lb, accb) = state
    return jnp.where(lo, acca / la, accb / lb)


def _lam(lam_ref, lam_init):
    a = lam_ref[...]
    t1 = jnp.sum(a[0:1] * a[1:2], axis=-1, keepdims=True)
    t2 = jnp.sum(a[2:3] * a[3:4], axis=-1, keepdims=True)
    return jnp.exp(t1) - jnp.exp(t2) + lam_init


def _diff_norm(o, lo, gain, lam_init):
    ms = _half_sums(o * o, lo) * (1.0 / HEAD_DIM)
    return (o * lax.rsqrt(ms + EPS) * gain) * (1.0 - lam_init)


def _diff_out(state, lo, lam, gain, lam_init):
    (_, l0, a0), (_, l1, a1), (_, l2, a2), (_, l3, a3) = state
    o = jnp.where(lo, a0 / l0, a2 / l2) - lam * jnp.where(lo, a1 / l1, a3 / l3)
    return _diff_norm(o, lo, gain, lam_init)


def _rows(r0, n):
    return slice(r0, r0 + n)


def _tile_update(scr, q_scr, k, v, tq, key_bias_fn, row_bias_scr, visibility):
    s_scr, p_scr, m_scr, c_scr, a_scr, acc_scr = scr
    n_rows, tk = s_scr.shape
    n_chunks = tk // LANES
    chunk = lambda j: slice(j * LANES, (j + 1) * LANES)

    def key_extent(r0):
        seen = [j for j in range(n_chunks) for r in range(r0 % tq, r0 % tq + GROUP_ROWS, STRIP)
                if visibility(r, j) is not None]
        return (max(seen) + 1) * LANES

    for r0 in range(0, n_rows, GROUP_ROWS):
        rows = _rows(r0, GROUP_ROWS)
        n_keys = key_extent(r0)
        s_scr[rows, :n_keys] = _dot_nt(q_scr[rows, :], k[:n_keys])
    for r0 in range(0, n_rows, STRIP):
        rows = _rows(r0, STRIP)
        stream, row_in_stream = divmod(r0, tq)
        live = []
        for j in range(n_chunks):
            vis = visibility(row_in_stream, j)
            if vis is None:
                continue
            t = s_scr[rows, chunk(j)]
            if key_bias_fn is not None:
                t = t - key_bias_fn(stream, j)
            if vis is not True:
                t = jnp.where(vis, t, MASKED)
            if key_bias_fn is not None or vis is not True:
                s_scr[rows, chunk(j)] = t
            live.append(t)
        top = functools.reduce(jnp.maximum, live)
        top = jnp.broadcast_to(jnp.max(top, axis=-1, keepdims=True), (STRIP, LANES))
        m_old = m_scr[rows, :]
        if row_bias_scr is not None:
            row_bias = row_bias_scr[rows, :]
            m_new = jnp.maximum(m_old, top + row_bias)
            c_scr[rows, :] = m_new - row_bias
        else:
            m_new = jnp.maximum(m_old, top)
            c_scr[rows, :] = m_new
        a_scr[rows, :] = jnp.exp2(m_old - m_new)
        m_scr[rows, :] = m_new
    for r0 in range(0, n_rows, STRIP):
        rows = _rows(r0, STRIP)
        centre = c_scr[rows, :]
        for j in range(n_chunks):
            if visibility(r0 % tq, j) is None:
                p_scr[rows, chunk(j)] = jnp.zeros((STRIP, LANES), BF16)
            else:
                p_scr[rows, chunk(j)] = jnp.exp2((s_scr[rows, chunk(j)] - centre).astype(BF16))
    ones_col = (lax.broadcasted_iota(jnp.int32, v.shape, 1) == 0).astype(BF16)
    v_ext = jnp.concatenate([v, ones_col], axis=1)
    for r0 in range(0, n_rows, GROUP_ROWS):
        rows = _rows(r0, GROUP_ROWS)
        alpha = a_scr[rows, :]
        n_keys = key_extent(r0)
        acc_scr[rows, :] = (jnp.concatenate([alpha, alpha], axis=1) * acc_scr[rows, :]
                            + jnp.dot(p_scr[rows, :n_keys], v_ext[:n_keys],
                                      preferred_element_type=F32))


def _all_visible(row0, j):
    return True


def _causal_visibility(row0, j):
    if j * LANES > row0 + STRIP - 1:
        return None
    if j * LANES + LANES - 1 <= row0:
        return True
    rel = (lax.broadcasted_iota(jnp.int32, (STRIP, LANES), 1)
           - lax.broadcasted_iota(jnp.int32, (STRIP, LANES), 0))
    return rel <= row0 - j * LANES


def _chunk_visibility(row0, j):
    q_chunk = row0 // CHUNK
    first, last = j * (LANES // CHUNK), (j + 1) * (LANES // CHUNK) - 1
    if first > q_chunk:
        return None
    if last <= q_chunk:
        return True
    return (lax.broadcasted_iota(jnp.int32, (STRIP, LANES), 1) >> _log2(CHUNK)) <= q_chunk - first


def _init_att_state(m_scr, acc_scr):
    m_scr[...] = jnp.full(m_scr.shape, MASKED, F32)
    acc_scr[...] = jnp.zeros(acc_scr.shape, F32)


def _normalised(acc_scr):
    acc = acc_scr[...]
    return acc[:, :LANES] / acc[:, LANES:LANES + 1]


def _fox_kernel(q_ref, k_ref, v_ref, fq_ref, fk_ref, o_ref,
                q_scr, fq_scr, m_scr, c_scr, a_scr, acc_scr, s_scr, p_scr):
    tq = q_ref.shape[2]
    tk = s_scr.shape[1]
    qi = pl.program_id(1)
    scr = (s_scr, p_scr, m_scr, c_scr, a_scr, acc_scr)
    chunks = tk // LANES
    head_col = lax.broadcasted_iota(jnp.int32, fq_ref.shape[1:], 1)
    lo = _lower_half((tq, LANES))

    def pair(p, carry):
        q_scr[...] = _stack_masked(q_ref[0, p].astype(F32), 2)
        for half in range(2):
            col = jnp.sum(jnp.where(head_col == 2 * p + half, fq_ref[0], 0.0), axis=-1, keepdims=True)
            fq_scr[half * tq:(half + 1) * tq, :] = jnp.broadcast_to(col, (tq, LANES))
        _init_att_state(m_scr, acc_scr)

        def tile(ki, visibility):
            keys = pl.ds(pl.multiple_of(ki * tk, tk), tk)
            key_bias = lambda stream, j: fk_ref[0, 2 * p + stream, ki * chunks + j]
            _tile_update(scr, q_scr, k_ref[0, p, keys, :], v_ref[0, p, keys, :], tq,
                         key_bias, fq_scr, visibility)

        def full_tile(ki, c):
            tile(ki, _all_visible)
            return c

        lax.fori_loop(0, qi, full_tile, 0)
        tile(qi, _causal_visibility)
        o = _normalised(acc_scr)
        o_ref[0, p] = jnp.where(lo, o[:tq], o[tq:]).astype(o_ref.dtype)
        return carry

    lax.fori_loop(0, N_PAIRS, pair, 0)


def _diff_kernel(q_ref, k_ref, v_ref, lam_ref, g_ref, o_ref,
                 q_scr, m_scr, c_scr, a_scr, acc_scr, s_scr, p_scr, *, lam_init):
    tq = q_ref.shape[2]
    tk = s_scr.shape[1]
    qi = pl.program_id(1)
    scr = (s_scr, p_scr, m_scr, c_scr, a_scr, acc_scr)
    lo = _lower_half((tq, LANES))
    lam = _lam(lam_ref, lam_init)

    def pair(p, carry):
        q_scr[...] = _stack_masked(q_ref[0, p].astype(F32), 4)
        _init_att_state(m_scr, acc_scr)

        def tile(ki, visibility):
            keys = pl.ds(pl.multiple_of(ki * tk, tk), tk)
            _tile_update(scr, q_scr, k_ref[0, p, keys, :], v_ref[0, p, keys, :], tq,
                         None, None, visibility)

        def full_tile(ki, c):
            tile(ki, _all_visible)
            return c

        lax.fori_loop(0, qi, full_tile, 0)
        tile(qi, _chunk_visibility)
        o = _normalised(acc_scr)
        o = (jnp.where(lo, o[:tq], o[2 * tq:3 * tq]) - lam * jnp.where(lo, o[tq:2 * tq], o[3 * tq:]))
        o_ref[0, p] = _diff_norm(o, lo, g_ref[p], lam_init).astype(o_ref.dtype)
        return carry

    lax.fori_loop(0, N_PAIRS, pair, 0)


def _prompt_att_call(kernel_fn, name, n_streams, tile, q, k, v, extra, extra_specs, row_bias):
    b, _, l, _ = q.shape
    tq = tk = tile
    assert l % tq == 0 and tq % CHUNK == 0
    rows = n_streams * tq
    qspec = pl.BlockSpec((1, N_PAIRS, tq, LANES), lambda i, t: (i, 0, t, 0))
    kspec = pl.BlockSpec((1, N_PAIRS, l, LANES), lambda i, t: (i, 0, 0, 0), pipeline_mode=pl.Buffered(1))
    stats = pltpu.VMEM((rows, LANES), F32)
    return pl.pallas_call(
        kernel_fn, grid=(b, l // tq),
        in_specs=[qspec, kspec, kspec] + extra_specs,
        out_specs=qspec,
        out_shape=jax.ShapeDtypeStruct(q.shape, BF16),
        scratch_shapes=[pltpu.VMEM((rows, LANES), BF16)] + ([stats] if row_bias else [])
        + [stats, stats, stats, pltpu.VMEM((rows, 2 * LANES), F32),
           pltpu.VMEM((rows, tk), F32), pltpu.VMEM((rows, tk), BF16)],
        compiler_params=_params("arbitrary", "arbitrary"),
        name=name,
    )(q, k, v, *extra)


def _fox_prompt(q, k, v, fq, fk_rep):
    tq = min(FOX_TILE, q.shape[2])
    specs = [pl.BlockSpec((1, tq, fq.shape[-1]), lambda i, t: (i, t, 0)),
             pl.BlockSpec((1,) + fk_rep.shape[1:], lambda i, t: (i, 0, 0, 0, 0),
                          pipeline_mode=pl.Buffered(1))]
    return _prompt_att_call(_fox_kernel, "fox_attention", 2, tq, q, k, v, (fq, fk_rep), specs, True)


def _diff_prompt(q, k, v, lam_rows, gain, lam_init):
    gain_pairs = gain.reshape(N_PAIRS, 1, LANES)
    specs = [_resident(lam_rows.shape), _resident(gain_pairs.shape)]
    return _prompt_att_call(functools.partial(_diff_kernel, lam_init=lam_init), "diff_attention", 4,
                            min(DIFF_TILE, q.shape[2]), q, k, v, (lam_rows, gain_pairs), specs, False)


def _fox_sample_kernel(q_ref, kp_ref, vp_ref, kn_ref, vn_ref, fq_ref, fkp_ref, fkn_ref, o_ref):
    tq, tn = q_ref.shape[2], kn_ref.shape[2]
    fq, fkp, fkn = fq_ref[0], fkp_ref[0], fkn_ref[0]
    causal = lax.broadcasted_iota(jnp.int32, (tq, tn), 1) <= lax.broadcasted_iota(jnp.int32, (tq, tn), 0)
    lo = _lower_half((tq, LANES))
    for p in range(N_PAIRS):
        idx = (2 * p, 2 * p + 1)
        qf = q_ref[0, p].astype(F32)
        state = _pair_block(qf, _cached_pair(kp_ref, p), _cached_pair(vp_ref, p),
                            [fq[:, h:h + 1] - fkp[h:h + 1, :] for h in idx], None, _fresh(2, tq),
                            frames_last=True)
        state = _pair_block(qf, kn_ref[0, p], vn_ref[0, p],
                            [fq[:, h:h + 1] - fkn[h:h + 1, :] for h in idx], causal, state)
        o_ref[0, p] = _fox_out(state, lo).astype(o_ref.dtype)


def _diff_sample_kernel(q_ref, kp_ref, vp_ref, kn_ref, vn_ref, lam_ref, g_ref, o_ref,
                        *, lam_init, past_len, n_new):
    tq, tn = q_ref.shape[2], kn_ref.shape[2]
    col = lax.broadcasted_iota(jnp.int32, (tq, tn), 1)
    row = lax.broadcasted_iota(jnp.int32, (tq, tn), 0)
    shift = _log2(CHUNK)
    visible = (((past_len + col) >> shift) <= ((past_len + row) >> shift)) & (col < n_new)
    lo = _lower_half((tq, LANES))
    lam = _lam(lam_ref, lam_init)
    for p in range(N_PAIRS):
        sl = slice(p * LANES, (p + 1) * LANES)
        qf = q_ref[0, p].astype(F32)
        state = _pair_block(qf, _cached_pair(kp_ref, p), _cached_pair(vp_ref, p), None, None,
                            _fresh(4, tq), frames_last=True)
        state = _pair_block(qf, kn_ref[0, p], vn_ref[0, p], None, visible, state)
        o_ref[0, p] = _diff_out(state, lo, lam, g_ref[:, sl], lam_init).astype(o_ref.dtype)


def _sample_specs(q, k_past, k_new, layer):
    b = q.shape[0]
    pairs = lambda a: pl.BlockSpec((1,) + a.shape[1:], lambda i: (i, 0, 0, 0))
    past = pl.BlockSpec((1, 1) + k_past.shape[2:], lambda i: (layer, i, 0, 0, 0))
    return b, pairs(q), past, pairs(k_new)


def _fox_sample(q, k_past, v_past, k_new, v_new, fq, fk_past, fk_new, layer):
    b, qspec, pspec, nspec = _sample_specs(q, k_past, k_new, layer)
    frow = lambda a: pl.BlockSpec((1,) + a.shape[1:], lambda i: (i, 0, 0))
    return pl.pallas_call(
        _fox_sample_kernel, grid=(b,),
        in_specs=[qspec, pspec, pspec, nspec, nspec, frow(fq), frow(fk_past), frow(fk_new)],
        out_specs=qspec,
        out_shape=jax.ShapeDtypeStruct(q.shape, BF16),
        compiler_params=_params("arbitrary"),
        name="fox_attention_cached",
    )(q, k_past, v_past, k_new, v_new, fq, fk_past, fk_new)


def _diff_sample(q, k_past, v_past, k_new, v_new, lam_rows, gain, lam_init, n_new, layer):
    b, qspec, pspec, nspec = _sample_specs(q, k_past, k_new, layer)
    past_len = k_past.shape[-1]
    assert past_len % CHUNK == 0
    return pl.pallas_call(
        functools.partial(_diff_sample_kernel, lam_init=lam_init, past_len=past_len, n_new=n_new),
        grid=(b,),
        in_specs=[qspec, pspec, pspec, nspec, nspec, _resident(lam_rows.shape), _resident((1, GROUP_W))],
        out_specs=qspec,
        out_shape=jax.ShapeDtypeStruct(q.shape, BF16),
        compiler_params=_params("arbitrary"),
        name="diff_attention_cached",
    )(q, k_past, v_past, k_new, v_new, lam_rows, gain)


def _mlp_kernel(x_ref, r_ref, f_ref, d_ref, g1_ref, sc_ref, sh_ref, g2_ref, ng_ref,
                wo_ref, wu_ref, wd_ref, fg_ref, o_ref, *, final_norm):
    bt, tl, d = x_ref.shape
    m = bt * tl
    mix = jnp.dot(r_ref[...].reshape(m, GROUP_W), wo_ref[0], preferred_element_type=F32)
    for i, ref in ((1, f_ref), (2, d_ref)):
        for p in range(N_PAIRS):
            mix = mix + jnp.dot(ref[:, p].reshape(m, LANES), wo_ref[i, p * LANES:(p + 1) * LANES, :],
                                preferred_element_type=F32)
    x = x_ref[...] + g1_ref[...] * mix.reshape(bt, tl, d)
    hb = _rms_mod(x, ng_ref[...], sc_ref[...], sh_ref[...]).reshape(m, d).astype(BF16)
    d_ff = wu_ref.shape[1]
    step = d_ff // 4
    y = jnp.zeros((m, d), F32)
    for j in range(0, d_ff, step):
        u = jnp.maximum(jnp.dot(hb, wu_ref[:, j:j + step], preferred_element_type=F32), 0.0)
        y = y + jnp.dot((u * u).astype(BF16), wd_ref[j:j + step, :], preferred_element_type=F32)
    x = x + g2_ref[...] * y.reshape(bt, tl, d)
    if final_norm:
        x = x * lax.rsqrt(jnp.mean(x * x, axis=-1, keepdims=True) + EPS) * fg_ref[...]
    o_ref[...] = x


def _mlp(x, r, f, dd, g1, sc2, sh2, g2, norm_g, w_out3, w_up, w_down, final_g, bt, tl, final_norm):
    b, l, d = x.shape
    tile = lambda w: pl.BlockSpec((bt, tl, w), lambda i, t: (i, t, 0))
    pairs = pl.BlockSpec((bt, N_PAIRS, tl, LANES), lambda i, t: (i, 0, t, 0))
    mod = pl.BlockSpec((bt, 1, d), lambda i, t: (i, 0, 0))
    return pl.pallas_call(
        functools.partial(_mlp_kernel, final_norm=final_norm),
        grid=(b // bt, l // tl),
        in_specs=[tile(d), tile(GROUP_W), pairs, pairs, mod, mod, mod, mod,
                  _resident((1, d)), _resident(w_out3.shape), _resident(w_up.shape),
                  _resident(w_down.shape), _resident((1, d))],
        out_specs=tile(d),
        out_shape=jax.ShapeDtypeStruct(x.shape, F32),
        compiler_params=_params("arbitrary", "arbitrary"),
        name="out_proj_mlp",
    )(x, r, f, dd, g1, sc2, sh2, g2, norm_g, w_out3, w_up, w_down, final_g)


def _rope_tables(pos, half, batch_reps):
    inv = ROPE_THETA ** (-jnp.arange(half, dtype=F32) / half)
    ang = pos.astype(F32)[:, None] * inv[None, :]
    cos, sin = jnp.cos(ang), jnp.sin(ang)
    reps = GROUP_W // (2 * half)
    cos = jnp.tile(jnp.concatenate([cos, cos], axis=-1), (batch_reps, reps))
    sin = jnp.tile(jnp.concatenate([-sin, sin], axis=-1), (batch_reps, reps))
    return cos, sin


def _pad_axis(a, axis, size):
    pad = [(0, 0)] * a.ndim
    pad[axis] = (0, size - a.shape[axis])
    return jnp.pad(a, pad)


def _forget_cumsum(logf_all):
    lk = logf_all.shape[1]
    x = jnp.transpose(logf_all, (0, 2, 1))
    x = _pad_axis(_pad_axis(x, 1, F_ROWS), 2, -(-lk // LANES) * LANES)
    return _cumsum(x)


def _group_layer(x, mods, tabs, past, lw, lam_init, bt, tl, final_g, final_norm, layer, depth,
                 cache_rows):
    b, l, d = x.shape
    sh1, sc1, g1, sh2, sc2, g2 = mods
    (rq, rk, rv, rg, fq, fk_f, fk_b, fv_f, fv_b, logf_pad,
     dq, dk_f, dk_b, dv_f, dv_b) = _inproj(x, sc1, sh1, lw["norm1_g"], lw["w_main"], lw["w_ff"],
                                           lw["b_ff"], tabs, bt, tl, layer, depth, cache_rows)
    logf = logf_pad[:, :, :N_HEADS]
    if past is None:
        s0 = jnp.zeros((b, N_HEADS, HEAD_DIM, HEAD_DIM), F32)
        f_t = _forget_cumsum(logf)
        fq_cum = jnp.transpose(f_t[:, :N_HEADS + 2, :l], (0, 2, 1))
        fk_rep = jnp.broadcast_to(f_t[:, :N_HEADS].reshape(b, N_HEADS, l // LANES, 1, LANES),
                                  (b, N_HEADS, l // LANES, STRIP, LANES))
        f_out = _fox_prompt(fq, fk_b, fv_b, fq_cum, fk_rep)
        d_out = _diff_prompt(dq, dk_b, dv_b, lw["lam_rows"], lw["diff_norm_g"], lam_init)
    else:
        pfk, pfv, plogf, pdk, pdv, s0 = past
        past_len = pfk.shape[-1]
        f_t = _forget_cumsum(jnp.concatenate([plogf, logf], axis=1))
        fq_cum = jnp.transpose(f_t[:, :N_HEADS + 2, past_len:past_len + l], (0, 2, 1))
        new_rows = lambda a: _pad_axis(a, 2, LANES)
        f_out = _fox_sample(fq, pfk, pfv, new_rows(fk_b), new_rows(fv_b), fq_cum,
                            f_t[:, :, :past_len], f_t[:, :, past_len:past_len + LANES], layer)
        d_out = _diff_sample(dq, pdk, pdv, new_rows(dk_b), new_rows(dv_b), lw["lam_rows"],
                             lw["diff_norm_g"], lam_init, l, layer)
    r_out, s_pairs = _retention(rq, rk, rv, rg, _state_to_pairs(s0), lw["ret_norm_g"],
                                lw["ret_norm_b"], bt, tl, min(CHUNK, l))
    x = _mlp(x, r_out, f_out, d_out, g1, sc2, sh2, g2, lw["norm2_g"], lw["w_out3"], lw["w_up"],
             lw["w_down"], final_g, bt, tl, final_norm)
    return x, (fk_f, fv_f, dk_f, dv_f), logf, _pairs_to_state(s_pairs)


def kernel(x_prompt, x_sample, cache_fox_k, cache_fox_v, cache_fox_logf, cache_diff_k, cache_diff_v,
           state_ret, c_prompt, c_sample, norm1_g, norm2_g, w_ada, b_ada, w_in, b_forget,
           ret_norm_g, ret_norm_b, lam_q1, lam_k1, lam_q2, lam_k2, diff_norm_g, w_out, w_up, w_down,
           final_g):
    depth, d = norm1_g.shape
    bp, lp, _ = x_prompt.shape
    bs, ls, _ = x_sample.shape
    past_len = cache_fox_k.shape[2]

    rows = -(-(bp + bs) // 8) * 8
    c_all = _pad_axis(jnp.concatenate([c_prompt, c_sample], axis=0), 0, rows)
    mod = _ada(c_all, w_ada, b_ada)

    def mods(li, r0, n):
        return [mod[li, r0:r0 + n, i * d:(i + 1) * d].reshape(n, 1, d) for i in range(6)]

    ff0 = 7 * GROUP_W
    w_main = jnp.concatenate([w_in[:, :, :ff0], w_in[:, :, ff0 + N_HEADS:]], axis=-1).astype(BF16)
    w_ff = _pad_axis(w_in[:, :, ff0:ff0 + N_HEADS], 2, LANES).astype(BF16)
    b_ff = _pad_axis(b_forget, 1, LANES).reshape(depth, 1, LANES)
    w_out3 = w_out.reshape(depth, 3, GROUP_W, d).astype(BF16)
    w_up_b, w_down_b = w_up.astype(BF16), w_down.astype(BF16)
    lam_rows = _pad_axis(_pad_axis(jnp.stack([lam_q1, lam_k1, lam_q2, lam_k2], axis=1), 2, LANES), 1, 8)
    final_row = final_g.reshape(1, d)

    tl_p = min(TOK_TILE, lp)
    p_pos = jnp.arange(lp, dtype=jnp.int32)
    s_pos = past_len + jnp.arange(ls, dtype=jnp.int32)
    tabs_p = _rope_tables(p_pos, HEAD_DIM // 2, 1) + _rope_tables(p_pos, DIFF_QK // 2, 1)
    tabs_s = _rope_tables(s_pos, HEAD_DIM // 2, bs) + _rope_tables(s_pos, DIFF_QK // 2, bs)

    frames_last = lambda a: jnp.transpose(a, (0, 1, 3, 4, 2))
    fox_k_t, fox_v_t = frames_last(cache_fox_k), frames_last(cache_fox_v)
    diff_k_t, diff_v_t = frames_last(cache_diff_k), frames_last(cache_diff_v)

    xp, xs = x_prompt, x_sample
    p_rows, s_rows = None, None
    p_logf, p_state, s_logf, s_state = [], [], [], []
    for li in range(depth):
        lam_init = 0.8 - 0.6 * math.exp(-0.3 * li)
        lw = dict(norm1_g=norm1_g[li].reshape(1, d), norm2_g=norm2_g[li].reshape(1, d),
                  w_main=w_main[li], w_ff=w_ff[li], b_ff=b_ff[li],
                  ret_norm_g=ret_norm_g[li].reshape(1, GROUP_W), ret_norm_b=ret_norm_b[li].reshape(1, GROUP_W),
                  lam_rows=lam_rows[li], diff_norm_g=diff_norm_g[li].reshape(1, GROUP_W),
                  w_out3=w_out3[li], w_up=w_up_b[li], w_down=w_down_b[li])
        last = li == depth - 1
        xp, p_rows, logf, state = _group_layer(xp, mods(li, 0, bp), tabs_p, None, lw, lam_init, 1, tl_p,
                                               final_row, last, li, depth, p_rows)
        p_logf.append(logf)
        p_state.append(state)
        past = (fox_k_t, fox_v_t, cache_fox_logf[li], diff_k_t, diff_v_t, state_ret[li])
        xs, s_rows, logf, state = _group_layer(xs, mods(li, bp, bs), tabs_s, past, lw, lam_init, bs, ls,
                                               final_row, last, li, depth, s_rows)
        s_logf.append(logf)
        s_state.append(state)

    def group_outputs(rows, logf, state):
        fk, fv, dk, dv = (a.reshape(a.shape[:3] + (N_HEADS, HEAD_DIM)) for a in rows)
        return fk, fv, jnp.stack(logf), dk, dv, jnp.stack(state)

    return (xp, xs) + group_outputs(p_rows, p_logf, p_state) + group_outputs(s_rows, s_logf, s_state)
```
